```python
import jax, jax.numpy as jnp
from jax import lax
import numpy as np

D_MODEL = 1024
BATCH = 2
SEQ = 8192
DEPTH = 4
DEC_BATCH = 32
DEC_SEQ = 1
PAST_LEN = 8192
PAGE_SIZE = 128

HEAD_DIM = 64
N_ATT_HEADS = D_MODEL // (2 * HEAD_DIM)
N_RWKV_HEADS = D_MODEL // HEAD_DIM - N_ATT_HEADS
ATT_W = N_ATT_HEADS * HEAD_DIM
RWKV_W = N_RWKV_HEADS * HEAD_DIM
LORA_DECAY = 64
LORA_ICLR = 64
LORA_GATE = 128
LORA_VRES = 32
ATT_COLS = 3 * ATT_W
RWKV_COLS = 3 * RWKV_W + LORA_DECAY + LORA_ICLR + LORA_GATE
DILATED_PATTERNS = ((128, 1), (512, 4), (2048, 16))
WINDOW_MAX = max(w for w, _ in DILATED_PATTERNS)
BAND_BLOCK = 128
ROPE_THETA = 10000.0
D_FF = 4 * D_MODEL
DEEPNORM_ALPHA = (2.0 * DEPTH) ** 0.25
DEEPNORM_BETA = (8.0 * DEPTH) ** -0.25
LN_EPS = 1e-5
GN_EPS = 64e-5
RMS_EPS = 1e-6
F32 = jnp.float32

kernel_name = "hymba_rwkv7_dilated_swa_deepnorm_step"


def layer_norm(x, g, b):
    xf = x.astype(F32)
    mu = xf.mean(-1, keepdims=True)
    var = jnp.square(xf - mu).mean(-1, keepdims=True)
    return ((xf - mu) * lax.rsqrt(var + LN_EPS) * g.astype(F32) + b.astype(F32)).astype(x.dtype)


def rope(x, pos):
    half = HEAD_DIM // 2
    inv = ROPE_THETA ** (-jnp.arange(half, dtype=F32) * (2.0 / HEAD_DIM))
    ang = pos.astype(F32)[:, None] * inv[None, :]
    cos = jnp.cos(ang)[None, :, None, :]
    sin = jnp.sin(ang)[None, :, None, :]
    xf = x.astype(F32)
    x1, x2 = xf[..., :half], xf[..., half:]
    return jnp.concatenate([x1 * cos - x2 * sin, x1 * sin + x2 * cos], -1).astype(x.dtype)


def combine_dilations(outs, lses):
    wts = jax.nn.softmax(jnp.stack(lses, 0), axis=0)
    return jnp.einsum('pbth,pbthd->bthd', wts, jnp.stack(outs, 0))


def dilated_band_attention(q, k, v, dilation, n_back):
    B, T, H, Dh = q.shape
    span = dilation * BAND_BLOCK
    Tp = -(-T // span) * span
    M = Tp // dilation
    nb = M // BAND_BLOCK

    def to_blocks(a):
        a = jnp.pad(a, ((0, 0), (0, Tp - T), (0, 0), (0, 0)))
        a = a.reshape(B, M, dilation, H, Dh).transpose(0, 2, 1, 3, 4)
        return a.reshape(B, dilation, nb, BAND_BLOCK, H, Dh)

    def with_prev(a):
        prev = jnp.pad(a[:, :, :-1], ((0, 0), (0, 0), (1, 0), (0, 0), (0, 0), (0, 0)))
        return jnp.concatenate([prev, a], axis=3)

    qb = to_blocks(q)
    kb = with_prev(to_blocks(k))
    vb = with_prev(to_blocks(v))
    s = jnp.einsum('brnqhd,brnkhd->brnhqk', qb, kb, preferred_element_type=F32) * (HEAD_DIM ** -0.5)
    qi = jnp.arange(BAND_BLOCK)
    ki = jnp.arange(2 * BAND_BLOCK) - BAND_BLOCK
    dist = qi[:, None] - ki[None, :]
    key_m = jnp.arange(nb)[:, None] * BAND_BLOCK + ki[None, :]
    valid = ((dist >= 0) & (dist <= n_back))[None] & (key_m >= 0)[:, None, :]
    s = jnp.where(valid[None, None, :, None], s, -jnp.inf)
    mx = s.max(-1, keepdims=True)
    e = jnp.exp(s - mx)
    den = e.sum(-1, keepdims=True)
    o = jnp.einsum('brnhqk,brnkhd->brnhqd', e, vb.astype(F32)) / den
    lse = (mx + jnp.log(den))[..., 0]
    o = o.transpose(0, 1, 2, 4, 3, 5).reshape(B, dilation, M, H, Dh)
    o = o.transpose(0, 2, 1, 3, 4).reshape(B, Tp, H, Dh)[:, :T]
    lse = lse.transpose(0, 1, 2, 4, 3).reshape(B, dilation, M, H)
    lse = lse.transpose(0, 2, 1, 3).reshape(B, Tp, H)[:, :T]
    return o, lse


def dilated_attention_prompt(q, k, v):
    outs, lses = [], []
    for window, dilation in DILATED_PATTERNS:
        o, l = dilated_band_attention(q, k, v, dilation, window // dilation)
        outs.append(o)
        lses.append(l)
    return combine_dilations(outs, lses)


def dilated_attention_decode(q, k_new, v_new, k_buf, v_buf):
    L = k_buf.shape[1]
    S = q.shape[1]
    k_ext = jnp.concatenate([k_buf.astype(k_new.dtype), k_new], axis=1)
    v_ext = jnp.concatenate([v_buf.astype(v_new.dtype), v_new], axis=1)
    outs, lses = [], []
    for window, dilation in DILATED_PATTERNS:
        offs = jnp.arange(window // dilation + 1) * dilation
        idx = L + jnp.arange(S)[:, None] - offs[None, :]
        valid = idx >= 0
        idx = jnp.maximum(idx, 0)
        kg = jnp.take(k_ext, idx, axis=1)
        vg = jnp.take(v_ext, idx, axis=1)
        s = jnp.einsum('bshd,bsjhd->bshj', q, kg, preferred_element_type=F32) * (HEAD_DIM ** -0.5)
        s = jnp.where(valid[None, :, None, :], s, -jnp.inf)
        mx = s.max(-1, keepdims=True)
        e = jnp.exp(s - mx)
        den = e.sum(-1, keepdims=True)
        outs.append(jnp.einsum('bshj,bsjhd->bshd', e, vg.astype(F32)) / den)
        lses.append((mx + jnp.log(den))[..., 0])
    return combine_dilations(outs, lses)


def token_shift(p, p_prev0, mu):
    prev = jnp.concatenate([p_prev0[:, None], p[:, :-1]], axis=1)
    return p + (prev - p) * mu


def rwkv7_time_mix(z, v_first, wkv0, lp):
    B, T, _ = z.shape
    H, N = N_RWKV_HEADS, HEAD_DIM
    z = z.astype(F32)
    r = z[..., :RWKV_W]
    k = z[..., RWKV_W:2 * RWKV_W]
    v = z[..., 2 * RWKV_W:3 * RWKV_W]
    o = 3 * RWKV_W
    w_lo = z[..., o:o + LORA_DECAY]
    o += LORA_DECAY
    a_lo = z[..., o:o + LORA_ICLR]
    o += LORA_ICLR
    g_lo = z[..., o:o + LORA_GATE]
    w = -jax.nn.softplus(-(lp['decay_base'].astype(F32) + jnp.tanh(w_lo) @ lp['decay_up'].astype(F32))) - 0.5
    decay = jnp.exp(-jnp.exp(w))
    a = jax.nn.sigmoid(lp['iclr_base'].astype(F32) + a_lo @ lp['iclr_up'].astype(F32))
    g = jax.nn.sigmoid(g_lo) @ lp['gate_up'].astype(F32)
    if v_first is None:
        v_first = v
    else:
        vres_lo = z[..., RWKV_COLS:]
        v = v + (v_first - v) * jax.nn.sigmoid(lp['vres_base'].astype(F32) + vres_lo @ lp['vres_up'].astype(F32))
    kk = (k * lp['key_scale_k'].astype(F32)).reshape(B, T, H, N)
    kk = kk / jnp.maximum(jnp.sqrt(jnp.sum(kk * kk, -1, keepdims=True)), 1e-12)
    k = k * (1.0 + (a - 1.0) * lp['key_scale_a'].astype(F32))
    rh, kh, vh = (t.reshape(B, T, H, N) for t in (r, k, v))
    dh, ah = decay.reshape(B, T, H, N), a.reshape(B, T, H, N)

    def step(S, inp):
        r_t, w_t, k_t, v_t, a_t, b_t = inp
        sa = jnp.einsum('bhvk,bhk->bhv', S, a_t)
        S = S * w_t[:, :, None, :] + sa[..., None] * b_t[:, :, None, :] + v_t[..., None] * k_t[:, :, None, :]
        return S, jnp.einsum('bhvk,bhk->bhv', S, r_t)

    xs = tuple(jnp.moveaxis(t, 1, 0) for t in (rh, dh, kh, vh, -kk, kk * ah))
    S_T, y = lax.scan(step, wkv0.astype(F32), xs)
    y = jnp.moveaxis(y, 0, 1)
    mu = y.mean(-1, keepdims=True)
    var = jnp.square(y - mu).mean(-1, keepdims=True)
    y = ((y - mu) * lax.rsqrt(var + GN_EPS)).reshape(B, T, RWKV_W)
    y = y * lp['gn_g'].astype(F32) + lp['gn_b'].astype(F32)
    bonus = jnp.sum(rh * kh * lp['bonus_rk'].astype(F32).reshape(H, N), -1, keepdims=True) * vh
    y = (y + bonus.reshape(B, T, RWKV_W)) * g
    return y, v_first, S_T


def hybrid_layer(x, pos, x_prev, wkv0, kv_buf, v_first, lp):
    B, T, _ = x.shape
    w_comb = lp['w_in']
    p = x @ w_comb
    q = p[..., :ATT_W].reshape(B, T, N_ATT_HEADS, HEAD_DIM)
    k = p[..., ATT_W:2 * ATT_W].reshape(B, T, N_ATT_HEADS, HEAD_DIM)
    v = p[..., 2 * ATT_W:ATT_COLS].reshape(B, T, N_ATT_HEADS, HEAD_DIM)
    q, k = rope(q, pos), rope(k, pos)
    if kv_buf is None:
        att = dilated_attention_prompt(q, k, v)
    else:
        att = dilated_attention_decode(q, k, v, kv_buf[0], kv_buf[1])
    att = att.reshape(B, T, ATT_W)
    att = att * lax.rsqrt(jnp.mean(jnp.square(att), -1, keepdims=True) + RMS_EPS) * lp['att_gain'].astype(F32)
    pr = p[..., ATT_COLS:]
    if x_prev is None:
        prev0 = jnp.zeros((B, pr.shape[-1]), pr.dtype)
    else:
        prev0 = x_prev.astype(x.dtype) @ w_comb[:, ATT_COLS:]
    z = token_shift(pr, prev0, lp['shift_mu'])
    rw, v_first, S_T = rwkv7_time_mix(z, v_first, wkv0, lp)
    mix = jnp.concatenate([att.astype(x.dtype), rw.astype(x.dtype)], -1) @ lp['w_out']
    x1 = layer_norm(DEEPNORM_ALPHA * x + mix, lp['ln1_g'], lp['ln1_b'])
    h = jnp.square(jax.nn.relu(x1 @ lp['w_ff_up'])) @ lp['w_ff_down']
    y = layer_norm(DEEPNORM_ALPHA * x1 + h, lp['ln2_g'], lp['ln2_b'])
    return y, x[:, -1], S_T, k, v, v_first


def setup_inputs(seed: int = 0) -> dict:
    key = jax.random.key(seed)
    ks = iter(jax.random.split(key, 40))

    def nrm(shape, scale):
        return jax.random.normal(next(ks), shape, F32) * scale

    def unif(shape, lo, hi):
        return jax.random.uniform(next(ks), shape, F32, lo, hi)

    kv_len = min(WINDOW_MAX, PAST_LEN)
    p_main = ATT_COLS + RWKV_COLS
    return {
        "x_prompt": nrm((BATCH, SEQ, D_MODEL), 1.0),
        "x_sample": nrm((DEC_BATCH, DEC_SEQ, D_MODEL), 1.0),
        "state_shift": nrm((DEPTH, DEC_BATCH, D_MODEL), 1.0),
        "state_wkv": nrm((DEPTH, DEC_BATCH, N_RWKV_HEADS, HEAD_DIM, HEAD_DIM), 0.5),
        "cache_k": nrm((DEPTH, DEC_BATCH, kv_len, N_ATT_HEADS, HEAD_DIM), 1.0),
        "cache_v": nrm((DEPTH, DEC_BATCH, kv_len, N_ATT_HEADS, HEAD_DIM), 1.0),
        "w_in": nrm((DEPTH, D_MODEL, p_main), D_MODEL ** -0.5),
        "w_vres_in": nrm((DEPTH - 1, D_MODEL, LORA_VRES), D_MODEL ** -0.5),
        "shift_mu": unif((DEPTH, RWKV_COLS), 0.1, 0.9),
        "vres_mu": unif((DEPTH - 1, LORA_VRES), 0.1, 0.9),
        "decay_base": unif((DEPTH, RWKV_W), -6.0, 0.0),
        "decay_up": nrm((DEPTH, LORA_DECAY, RWKV_W), 0.05),
        "iclr_base": nrm((DEPTH, RWKV_W), 0.2),
        "iclr_up": nrm((DEPTH, LORA_ICLR, RWKV_W), 0.5 * LORA_ICLR ** -0.5),
        "gate_up": nrm((DEPTH, LORA_GATE, RWKV_W), LORA_GATE ** -0.5),
        "vres_base": 1.0 + nrm((DEPTH - 1, RWKV_W), 0.1),
        "vres_up": nrm((DEPTH - 1, LORA_VRES, RWKV_W), 0.1),
        "key_scale_k": 0.85 + nrm((DEPTH, RWKV_W), 0.05),
        "key_scale_a": 1.0 + nrm((DEPTH, RWKV_W), 0.05),
        "bonus_rk": nrm((DEPTH, RWKV_W), 0.1),
        "gn_g": 1.0 + nrm((DEPTH, RWKV_W), 0.05),
        "gn_b": nrm((DEPTH, RWKV_W), 0.02),
        "att_gain": 1.0 + nrm((DEPTH, ATT_W), 0.05),
        "w_out": nrm((DEPTH, D_MODEL, D_MODEL), DEEPNORM_BETA * D_MODEL ** -0.5),
        "ln1_g": 1.0 + nrm((DEPTH, D_MODEL), 0.05),
        "ln1_b": nrm((DEPTH, D_MODEL), 0.02),
        "w_ff_up": nrm((DEPTH, D_MODEL, D_FF), D_MODEL ** -0.5),
        "w_ff_down": nrm((DEPTH, D_FF, D_MODEL), DEEPNORM_BETA * D_FF ** -0.5),
        "ln2_g": 1.0 + nrm((DEPTH, D_MODEL), 0.05),
        "ln2_b": nrm((DEPTH, D_MODEL), 0.02),
    }


def reference(x_prompt, x_sample, state_shift, state_wkv, cache_k, cache_v, w_in, w_vres_in, shift_mu,
              vres_mu, decay_base, decay_up, iclr_base, iclr_up, gate_up, vres_base, vres_up, key_scale_k,
              key_scale_a, bonus_rk, gn_g, gn_b, att_gain, w_out, ln1_g, ln1_b, w_ff_up, w_ff_down, ln2_g,
              ln2_b):
    T_p = x_prompt.shape[1]
    pos_prompt = jnp.arange(T_p)
    pos_sample = PAST_LEN + jnp.arange(x_sample.shape[1])
    keep = min(WINDOW_MAX, T_p)
    wkv_zero = jnp.zeros((x_prompt.shape[0], N_RWKV_HEADS, HEAD_DIM, HEAD_DIM), F32)
    hp, hs = x_prompt, x_sample
    vf_p, vf_s = None, None
    shift_p, shift_s, wkv_p, wkv_s, kp_l, vp_l, ks_l, vs_l = [], [], [], [], [], [], [], []
    for l in range(DEPTH):
        lp = dict(w_in=w_in[l], shift_mu=shift_mu[l], decay_base=decay_base[l], decay_up=decay_up[l],
                  iclr_base=iclr_base[l], iclr_up=iclr_up[l], gate_up=gate_up[l],
                  key_scale_k=key_scale_k[l], key_scale_a=key_scale_a[l], bonus_rk=bonus_rk[l],
                  gn_g=gn_g[l], gn_b=gn_b[l], att_gain=att_gain[l], w_out=w_out[l],
                  ln1_g=ln1_g[l], ln1_b=ln1_b[l], w_ff_up=w_ff_up[l], w_ff_down=w_ff_down[l],
                  ln2_g=ln2_g[l], ln2_b=ln2_b[l])
        if l > 0:
            lp['w_in'] = jnp.concatenate([w_in[l], w_vres_in[l - 1]], axis=1)
            lp['shift_mu'] = jnp.concatenate([shift_mu[l], vres_mu[l - 1]], axis=0)
            lp['vres_base'] = vres_base[l - 1]
            lp['vres_up'] = vres_up[l - 1]
        hp, sp, Sp, kp, vp, vf_p = hybrid_layer(hp, pos_prompt, None, wkv_zero, None, vf_p, lp)
        hs, ss, Ss, kss, vss, vf_s = hybrid_layer(hs, pos_sample, state_shift[l], state_wkv[l],
                                                  (cache_k[l], cache_v[l]), vf_s, lp)
        shift_p.append(sp)
        shift_s.append(ss)
        wkv_p.append(Sp)
        wkv_s.append(Ss)
        kp_l.append(kp[:, T_p - keep:])
        vp_l.append(vp[:, T_p - keep:])
        ks_l.append(kss)
        vs_l.append(vss)
    return (hp, hs, jnp.stack(shift_p), jnp.stack(shift_s), jnp.stack(wkv_p), jnp.stack(wkv_s),
            jnp.stack(kp_l), jnp.stack(vp_l), jnp.stack(ks_l), jnp.stack(vs_l))
```

```python
import functools

import jax
import jax.numpy as jnp
from jax import lax
from jax.experimental import pallas as pl
from jax.experimental.pallas import tpu as pltpu

F32 = jnp.float32
BF16 = jnp.bfloat16

D_MODEL = 1024
DEPTH = 4
HEAD_DIM = 64
N_HEADS = 8
HW = N_HEADS * HEAD_DIM
LORA_DECAY, LORA_ICLR, LORA_GATE, LORA_VRES = 64, 64, 128, 32
LO_W = 384
PR_W = 3 * HW + LO_W
P_EXT = 3 * HW + PR_W
DILATIONS = (1, 4, 16)
N_BACK = 128
BLK = 128
ST_W = HW + 128
ROPE_THETA = 10000.0
D_FF = 4 * D_MODEL
ALPHA = (2.0 * DEPTH) ** 0.25
LN_EPS = 1e-5
GN_EPS = 64e-5
RMS_EPS = 1e-6
CHUNK = 64
VMEM_LIMIT = 56 * 1024 * 1024


def _cparams(*sem):
    return pltpu.CompilerParams(dimension_semantics=sem, vmem_limit_bytes=VMEM_LIMIT)


def _mm(a, b):
    return jnp.dot(a.astype(BF16), b.astype(BF16), preferred_element_type=F32)


def _mm_nt(a, b):
    return lax.dot_general(a.astype(BF16), b.astype(BF16), (((1,), (1,)), ((), ())),
                           preferred_element_type=F32)


def _split2(x):
    hi = x.astype(BF16)
    lo = (x - hi.astype(F32)).astype(BF16)
    return hi, lo


def _mm_hi(a, b):
    ah, al = _split2(a)
    bh, bl = _split2(b)
    d = functools.partial(jnp.dot, preferred_element_type=F32)
    return d(ah, bh) + d(ah, bl) + d(al, bh)


def _sum_heads(x, bd):
    hi, lo = _split2(x)
    d = functools.partial(jnp.dot, preferred_element_type=F32)
    return d(hi, bd) + d(lo, bd)


def _layer_norm(x, g, b):
    mu = jnp.mean(x, axis=-1, keepdims=True)
    xc = x - mu
    var = jnp.mean(xc * xc, axis=-1, keepdims=True)
    return xc * lax.rsqrt(var + LN_EPS) * g + b


def _rope(p, cos, sin):
    lane = lax.broadcasted_iota(jnp.int32, p.shape, 1) % HEAD_DIM
    rot = jnp.where(lane < HEAD_DIM // 2, pltpu.roll(p, HW - HEAD_DIM // 2, 1),
                    pltpu.roll(p, HEAD_DIM // 2, 1))
    return p * cos + rot * sin


def _proj_kernel(x_ref, w_ref, cos_ref, sin_ref, q_ref, kb_ref, vb_ref, kf_ref, vf_ref, pr_ref):
    xb = x_ref[...].astype(BF16)
    cos = jnp.concatenate([cos_ref[...]] * 4, axis=1)
    sin = jnp.concatenate([sin_ref[...]] * 4, axis=1)
    d = functools.partial(jnp.dot, preferred_element_type=F32)
    q = _rope(d(xb, w_ref[:, 0:HW]), cos, sin) * (HEAD_DIM ** -0.5)
    q_ref[...] = q.astype(BF16)
    k = _rope(d(xb, w_ref[:, HW:2 * HW]), cos, sin)
    kf_ref[...] = k
    kb_ref[...] = k.astype(BF16)
    v = d(xb, w_ref[:, 2 * HW:3 * HW])
    vf_ref[...] = v
    vb_ref[...] = v.astype(BF16)
    pr_ref[...] = d(xb, w_ref[:, 3 * HW:P_EXT])


def _proj(x, w_ext, layer, cos, sin, tm):
    n = x.shape[0]
    t_tiles = cos.shape[0] // tm
    row = lambda i: (i, 0)
    return pl.pallas_call(
        _proj_kernel,
        grid=(n // tm,),
        in_specs=[pl.BlockSpec((tm, D_MODEL), row),
                  pl.BlockSpec((None, D_MODEL, P_EXT), lambda i: (layer, 0, 0)),
                  pl.BlockSpec((tm, 128), lambda i: (i % t_tiles, 0)),
                  pl.BlockSpec((tm, 128), lambda i: (i % t_tiles, 0))],
        out_specs=[pl.BlockSpec((tm, HW), row)] * 5 + [pl.BlockSpec((tm, PR_W), row)],
        out_shape=[jax.ShapeDtypeStruct((n, HW), BF16)] * 3 + [jax.ShapeDtypeStruct((n, HW), F32)] * 2
        + [jax.ShapeDtypeStruct((n, PR_W), F32)],
        compiler_params=_cparams("parallel"),
        name="proj",
    )(x, w_ext, cos, sin)


def _proj_rwkv_kernel(x_ref, w_ref, pr_ref):
    pr_ref[...] = jnp.dot(x_ref[...].astype(BF16), w_ref[:, 3 * HW:P_EXT], preferred_element_type=F32)


def _proj_rwkv(x, w_ext, layer):
    n = x.shape[0]
    return pl.pallas_call(
        _proj_rwkv_kernel,
        grid=(1,),
        in_specs=[pl.BlockSpec((n, D_MODEL), lambda i: (0, 0)),
                  pl.BlockSpec((None, D_MODEL, P_EXT), lambda i: (layer, 0, 0))],
        out_specs=pl.BlockSpec((n, PR_W), lambda i: (0, 0)),
        out_shape=jax.ShapeDtypeStruct((n, PR_W), F32),
        compiler_params=_cparams("arbitrary"),
        name="proj_prev",
    )(x, w_ext)


def _att_kernel(*refs, has_prev, is_last):
    q_ref, k_ref, v_ref, kp_ref, vp_ref = refs[:5]
    pos = 5
    st_ref = gain_ref = None
    if has_prev:
        st_ref = refs[pos]
        pos += 1
    if is_last:
        gain_ref = refs[pos]
        pos += 1
    out_ref = refs[pos]
    o_scr = refs[pos + 1] if is_last else None

    n = pl.program_id(2)
    q = q_ref[...]
    kc = k_ref[...]
    vc = v_ref[...]
    kp = kp_ref[...]
    vp = vp_ref[...]
    qi = lax.broadcasted_iota(jnp.int32, (BLK, BLK), 0)
    ki = lax.broadcasted_iota(jnp.int32, (BLK, BLK), 1)
    valid_c = ki <= qi
    valid_p = jnp.where(ki >= qi, jnp.where(n > 0, 1, 0), 0) > 0
    nt = (((1,), (1,)), ((), ()))
    if not is_last:
        out_ref[:, HW:ST_W] = jnp.zeros((BLK, ST_W - HW), F32)
    for h in range(N_HEADS):
        sl = slice(h * HEAD_DIM, (h + 1) * HEAD_DIM)
        s_c = jnp.where(valid_c, lax.dot_general(q[:, sl], kc[:, sl], nt, preferred_element_type=F32), -jnp.inf)
        s_p = jnp.where(valid_p, lax.dot_general(q[:, sl], kp[:, sl], nt, preferred_element_type=F32), -jnp.inf)
        m = jnp.maximum(jnp.max(s_c, axis=-1, keepdims=True), jnp.max(s_p, axis=-1, keepdims=True))
        e_c = jnp.exp(s_c - m)
        e_p = jnp.exp(s_p - m)
        l = jnp.sum(e_c, axis=-1, keepdims=True) + jnp.sum(e_p, axis=-1, keepdims=True)
        acc = (jnp.dot(e_c.astype(BF16), vc[:, sl], preferred_element_type=F32)
               + jnp.dot(e_p.astype(BF16), vp[:, sl], preferred_element_type=F32))
        if has_prev:
            m0 = st_ref[:, HW + h:HW + h + 1]
            l0 = st_ref[:, HW + N_HEADS + h:HW + N_HEADS + h + 1]
            mn = jnp.maximum(m0, m)
            a0 = jnp.exp(m0 - mn)
            a1 = jnp.exp(m - mn)
            acc = a0 * st_ref[:, sl] + a1 * acc
            l = a0 * l0 + a1 * l
            m = mn
        if is_last:
            o_scr[:, sl] = acc / l
        else:
            out_ref[:, sl] = acc
            out_ref[:, HW + h:HW + h + 1] = m
            out_ref[:, HW + N_HEADS + h:HW + N_HEADS + h + 1] = l
    if is_last:
        att = o_scr[...]
        ms = jnp.mean(att * att, axis=-1, keepdims=True)
        out_ref[...] = (att * lax.rsqrt(ms + RMS_EPS) * gain_ref[...]).astype(BF16)


def _att_pattern(q, kb, vb, state, gain, d, is_last):
    B, T, _ = q.shape
    M = T // d
    nb = M // BLK
    view = lambda a, w: a.reshape(B, M, d * w)
    blk = lambda w: pl.BlockSpec((None, BLK, w), lambda b, r, n: (b, n, r))
    has_prev = state is not None
    pblk = pl.BlockSpec((None, BLK, HW), lambda b, r, n: (b, jnp.maximum(n - 1, 0), r))
    args = [view(q, HW), view(kb, HW), view(vb, HW), view(kb, HW), view(vb, HW)]
    in_specs = [blk(HW)] * 3 + [pblk] * 2
    if has_prev:
        args.append(view(state, ST_W))
        in_specs.append(blk(ST_W))
    if is_last:
        args.append(gain)
        in_specs.append(pl.BlockSpec((1, HW), lambda b, r, n: (0, 0)))
    scratch = []
    if is_last:
        scratch.append(pltpu.VMEM((BLK, HW), F32))
        out_w, out_dt = HW, BF16
    else:
        out_w, out_dt = ST_W, F32
    out = pl.pallas_call(
        functools.partial(_att_kernel, has_prev=has_prev, is_last=is_last),
        grid=(B, d, nb),
        in_specs=in_specs,
        out_specs=blk(out_w),
        out_shape=jax.ShapeDtypeStruct((B, M, d * out_w), out_dt),
        scratch_shapes=scratch,
        compiler_params=_cparams("parallel", "parallel", "arbitrary"),
        name=f"att_d{d}",
    )(*args)
    return out.reshape(B, T, out_w)


def _att_prompt(q, kb, vb, gain):
    state = None
    for i, d in enumerate(DILATIONS):
        state = _att_pattern(q, kb, vb, state, gain, d, i == len(DILATIONS) - 1)
    return state


def _att_dec_kernel(q_ref, kn_ref, vn_ref, k1, k4, k16, v1, v4, v16, gain_ref, out_ref, o_scr):
    q = q_ref[...]
    q8 = jnp.broadcast_to(q, (8, HW))
    qf = q.astype(F32)
    knf = kn_ref[...].astype(F32)
    vnf = vn_ref[...].astype(F32)
    kcat = jnp.concatenate([k1[...], k4[...], k16[...]], axis=0).astype(BF16)
    vcat = jnp.concatenate([v1[...], v4[...], v16[...]], axis=0).astype(BF16)
    n_pat = float(len(DILATIONS))
    for h in range(N_HEADS):
        sl = slice(h * HEAD_DIM, (h + 1) * HEAD_DIM)
        s = lax.dot_general(q8[:, sl], kcat[:, sl], (((1,), (1,)), ((), ())),
                            preferred_element_type=F32)[0:1]
        s_new = jnp.sum(qf[:, sl] * knf[:, sl], axis=-1, keepdims=True)
        m = jnp.maximum(jnp.max(s, axis=-1, keepdims=True), s_new)
        e = jnp.exp(s - m)
        e_new = jnp.exp(s_new - m) * n_pat
        l = jnp.sum(e, axis=-1, keepdims=True) + e_new
        e8 = jnp.broadcast_to(e, (8, e.shape[1])).astype(BF16)
        acc = jnp.dot(e8, vcat[:, sl], preferred_element_type=F32)[0:1] + e_new * vnf[:, sl]
        o_scr[:, sl] = acc / l
    att = o_scr[...]
    ms = jnp.mean(att * att, axis=-1, keepdims=True)
    out_ref[...] = (att * lax.rsqrt(ms + RMS_EPS) * gain_ref[...]).astype(BF16)


def _att_decode(q, kn, vn, cache_k, cache_v, layer, gain):
    bd, L = cache_k.shape[1], cache_k.shape[2]
    row = pl.BlockSpec((None, 1, HW), lambda b: (b, 0, 0))
    args, specs = [q, kn, vn], [row, row, row]
    for c in (cache_k, cache_v):
        for d in DILATIONS:
            assert (L // d) % BLK == 0
            args.append(c.reshape(DEPTH, bd, L // d, d * HW))
            last = L // d // BLK - 1
            specs.append(pl.BlockSpec((None, None, BLK, HW), lambda b, last=last: (layer, b, last, 0)))
    args.append(gain)
    specs.append(pl.BlockSpec((1, HW), lambda b: (0, 0)))
    return pl.pallas_call(
        _att_dec_kernel,
        grid=(bd,),
        in_specs=specs,
        out_specs=row,
        out_shape=jax.ShapeDtypeStruct((bd, 1, HW), BF16),
        scratch_shapes=[pltpu.VMEM((1, HW), F32)],
        compiler_params=_cparams("parallel"),
        name="att_dec",
    )(*args)


def _softplus(x):
    return jnp.maximum(x, 0.0) + jnp.log(1.0 + jnp.exp(-jnp.abs(x)))


def _rwkv_pre(z, vfirst, prm, has_vres):
    (dbase, up1, ibase, gup, vbase, vup, ksk, ksa, bd) = prm
    r = z[:, 0:HW]
    k = z[:, HW:2 * HW]
    v = z[:, 2 * HW:3 * HW]
    lo1 = z[:, 3 * HW:3 * HW + 128]
    lane = lax.broadcasted_iota(jnp.int32, lo1.shape, 1)
    lo1 = jnp.where(lane < LORA_DECAY, jnp.tanh(lo1), lo1)
    up = _mm(lo1, up1)
    w = -_softplus(-(dbase + up[:, 0:HW])) - 0.5
    logw = -jnp.exp(w)
    iclr = jax.nn.sigmoid(ibase + up[:, HW:2 * HW])
    g = _mm(jax.nn.sigmoid(z[:, 3 * HW + 128:3 * HW + 256]), gup)
    if has_vres:
        mix = jax.nn.sigmoid(vbase + _mm(z[:, 3 * HW + 256:3 * HW + 384], vup))
        v = v + (vfirst - v) * mix
    kk = k * ksk
    nrm = jnp.sqrt(_sum_heads(kk * kk, bd))
    kk = kk / jnp.maximum(nrm, 1e-12)
    k = k * (1.0 + (iclr - 1.0) * ksa)
    return r, k, v, logw, -kk, kk * iclr, g


def _rwkv_post(y, r, k, v, g, brk, gng, gnb, bd):
    mu = _sum_heads(y, bd) * (1.0 / HEAD_DIM)
    yc = y - mu
    var = _sum_heads(yc * yc, bd) * (1.0 / HEAD_DIM)
    yn = yc * lax.rsqrt(var + GN_EPS) * gng + gnb
    bonus = _sum_heads(r * k * brk, bd) * v
    return (yn + bonus) * g


def _rwkv_chunk_head(at, rt, bt, kt, vv, pc, s0, masks):
    strict, incl, eye, levels = masks
    c = CHUNK
    q2 = jnp.concatenate([at, rt], axis=0)
    k2 = jnp.concatenate([bt, kt], axis=0)
    g = _mm_nt(q2, k2)
    a_ab = jnp.where(strict, g[0:c, 0:c], 0.0)
    a_ak = jnp.where(strict, g[0:c, c:2 * c], 0.0)
    a_rb = jnp.where(incl, g[c:2 * c, 0:c], 0.0)
    a_rk = jnp.where(incl, g[c:2 * c, c:2 * c], 0.0)
    t = eye + jnp.where(levels[0], a_ab, 0.0)
    for lv in levels[1:]:
        t = t + _mm(_mm(t, jnp.where(lv, a_ab, 0.0)), t)
    x1 = _mm(a_ak, vv)
    wz = _mm(t, jnp.concatenate([at, x1], axis=1))
    ry = _mm(a_rb, wz)
    rp = rt + ry[:, 0:c]
    y0 = ry[:, c:2 * c] + _mm(a_rk, vv)
    lhs = jnp.concatenate([wz, jnp.concatenate([jnp.zeros_like(vv), vv], axis=1)], axis=0)
    mn = _mm(lhs.T, k2)
    mt = (eye + mn[0:c]) * pc
    nt = mn[c:2 * c] * pc
    y = _mm_nt(rp, s0) + y0
    s1 = _mm_hi(s0, mt) + nt
    return y, s1


def _chunk_masks():
    ri = lax.broadcasted_iota(jnp.int32, (CHUNK, CHUNK), 0)
    ci = lax.broadcasted_iota(jnp.int32, (CHUNK, CHUNK), 1)
    strict = ri > ci
    incl = ri >= ci
    eye = jnp.where(ri == ci, 1.0, 0.0).astype(F32)
    levels = []
    size = 2
    while size <= CHUNK:
        same = (ri // size) == (ci // size)
        if size == 2:
            levels.append(same)
        else:
            inner = (ri // (size // 2)) == (ci // (size // 2))
            levels.append(jnp.where(same, jnp.where(inner, 0, 1), 0) > 0)
        size *= 2
    return strict, incl, eye, levels


def _rwkv_kernel(*refs, has_vres, tb):
    pr_ref = refs[0]
    pos = 1
    vf_ref = None
    if has_vres:
        vf_ref = refs[pos]
        pos += 1
    (mu_ref, dbase, up1, ibase, gup, vbase, vup, ksk, ksa, brk, gng, gnb, bd_ref, tri_ref) = refs[pos:pos + 14]
    pos += 14
    rw_ref = refs[pos]
    pos += 1
    vfo_ref = None
    if not has_vres:
        vfo_ref = refs[pos]
        pos += 1
    so_ref = refs[pos]
    pos += 1
    s_scr, carry, at_s, rt_s, bt_s, kt_s, v_s, pc_s, y_s = refs[pos:pos + 9]

    i = pl.program_id(1)

    @pl.when(i == 0)
    def _():
        s_scr[...] = jnp.zeros_like(s_scr)
        carry[...] = jnp.zeros_like(carry)

    p = pr_ref[...]
    row = lax.broadcasted_iota(jnp.int32, p.shape, 0)
    prev = jnp.where(row == 0, carry[0:1, :], pltpu.roll(p, 1, 0))
    carry[0:1, :] = p[tb - 1:tb, :]
    z = p + (prev - p) * mu_ref[...]
    bd = bd_ref[...]
    prm = (dbase[...], up1[...], ibase[...], gup[...], vbase[...], vup[...], ksk[...], ksa[...], bd)
    r, k, v, logw, a, b, g = _rwkv_pre(z, vf_ref[...] if has_vres else None, prm, has_vres)
    if not has_vres:
        vfo_ref[...] = v
    hi = logw.astype(BF16)
    r1 = logw - hi.astype(F32)
    mid = r1.astype(BF16)
    low = (r1 - mid.astype(F32)).astype(BF16)
    tri = tri_ref[...]
    d = functools.partial(jnp.dot, preferred_element_type=F32)
    cum = d(tri, hi) + d(tri, mid) + d(tri, low)
    ecum = jnp.exp(cum)
    einv = jnp.exp(-cum)
    at_s[...] = a * jnp.exp(cum - logw)
    rt_s[...] = r * ecum
    bt_s[...] = b * einv
    kt_s[...] = k * einv
    v_s[...] = v
    pc_s[...] = ecum
    masks = _chunk_masks()

    def chunk(c, _):
        r0 = pl.multiple_of(c * CHUNK, CHUNK)
        rows = pl.ds(r0, CHUNK)
        tail = pl.ds(pl.multiple_of(r0 + CHUNK - 8, 8), 8)
        for h in range(N_HEADS):
            sl = slice(h * HEAD_DIM, (h + 1) * HEAD_DIM)
            y, s1 = _rwkv_chunk_head(at_s[rows, sl], rt_s[rows, sl], bt_s[rows, sl], kt_s[rows, sl],
                                     v_s[rows, sl], pc_s[tail, sl][7:8], s_scr[h], masks)
            y_s[rows, sl] = y
            s_scr[h] = s1
        return 0

    lax.fori_loop(0, tb // CHUNK, chunk, 0)
    rw = _rwkv_post(y_s[...], r, k, v, g, brk[...], gng[...], gnb[...], bd)
    rw_ref[...] = rw.astype(BF16)
    so_ref[...] = s_scr[...]


def _rwkv_prompt(pr, vfirst, wts, tb):
    B, T, _ = pr.shape
    has_vres = vfirst is not None
    blk = lambda w: pl.BlockSpec((None, tb, w), lambda b, i: (b, i, 0))
    const = lambda a: pl.BlockSpec(a.shape, lambda b, i: (0,) * a.ndim)
    args, specs = [pr], [blk(PR_W)]
    if has_vres:
        args.append(vfirst)
        specs.append(blk(HW))
    idx = jnp.arange(tb)
    tri = ((idx[:, None] // CHUNK == idx[None, :] // CHUNK) & (idx[:, None] >= idx[None, :])).astype(BF16)
    for a in wts + (tri,):
        args.append(a)
        specs.append(const(a))
    out_shape = [jax.ShapeDtypeStruct((B, T, HW), BF16)]
    out_specs = [blk(HW)]
    if not has_vres:
        out_shape.append(jax.ShapeDtypeStruct((B, T, HW), F32))
        out_specs.append(blk(HW))
    out_shape.append(jax.ShapeDtypeStruct((B, N_HEADS, HEAD_DIM, HEAD_DIM), F32))
    out_specs.append(pl.BlockSpec((None, N_HEADS, HEAD_DIM, HEAD_DIM), lambda b, i: (b, 0, 0, 0)))
    big = lambda: pltpu.VMEM((tb, HW), F32)
    scratch = [pltpu.VMEM((N_HEADS, HEAD_DIM, HEAD_DIM), F32), pltpu.VMEM((8, PR_W), F32),
               big(), big(), big(), big(), big(), big(), big()]
    outs = pl.pallas_call(
        functools.partial(_rwkv_kernel, has_vres=has_vres, tb=tb),
        grid=(B, T // tb),
        in_specs=specs,
        out_specs=out_specs,
        out_shape=out_shape,
        scratch_shapes=scratch,
        compiler_params=_cparams("parallel", "arbitrary"),
        name="rwkv",
    )(*args)
    if has_vres:
        return outs[0], vfirst, outs[1]
    return outs[0], outs[1], outs[2]


def _rwkv_dec_kernel(*refs, has_vres):
    pr_ref, pv_ref = refs[0:2]
    pos = 2
    vf_ref = None
    if has_vres:
        vf_ref = refs[pos]
        pos += 1
    s_ref = refs[pos]
    pos += 1
    (mu_ref, dbase, up1, ibase, gup, vbase, vup, ksk, ksa, brk, gng, gnb, bd_ref) = refs[pos:pos + 13]
    pos += 13
    rw_ref = refs[pos]
    pos += 1
    vfo_ref = None
    if not has_vres:
        vfo_ref = refs[pos]
        pos += 1
    so_ref, y_s = refs[pos:pos + 2]

    p = pr_ref[...]
    z = p + (pv_ref[...] - p) * mu_ref[...]
    bd = bd_ref[...]
    prm = (dbase[...], up1[...], ibase[...], gup[...], vbase[...], vup[...], ksk[...], ksa[...], bd)
    r, k, v, logw, a, b, g = _rwkv_pre(z, vf_ref[...] if has_vres else None, prm, has_vres)
    if not has_vres:
        vfo_ref[...] = v
    w = jnp.exp(logw)
    ri = lax.broadcasted_iota(jnp.int32, (HEAD_DIM, HEAD_DIM), 0)
    ci = lax.broadcasted_iota(jnp.int32, (HEAD_DIM, HEAD_DIM), 1)
    eye = jnp.where(ri == ci, 1.0, 0.0).astype(F32)
    for h in range(N_HEADS):
        sl = slice(h * HEAD_DIM, (h + 1) * HEAD_DIM)
        s = s_ref[h]
        sa = jnp.sum(s * a[:, sl], axis=-1, keepdims=True)
        vcol = jnp.sum(eye * v[:, sl], axis=-1, keepdims=True)
        s = s * w[:, sl] + sa * b[:, sl] + vcol * k[:, sl]
        so_ref[h] = s
        ycol = jnp.sum(s * r[:, sl], axis=-1, keepdims=True)
        y_s[:, sl] = jnp.sum(eye * ycol, axis=0, keepdims=True)
    rw = _rwkv_post(y_s[...], r, k, v, g, brk[...], gng[...], gnb[...], bd)
    rw_ref[...] = rw.astype(BF16)


def _rwkv_decode(pr, prev, vfirst, state, layer, wts):
    bd_ = pr.shape[0]
    has_vres = vfirst is not None
    row = lambda w: pl.BlockSpec((None, 1, w), lambda b: (b, 0, 0))
    const = lambda a: pl.BlockSpec(a.shape, lambda b: (0,) * a.ndim)
    sblk = (None, N_HEADS, HEAD_DIM, HEAD_DIM)
    args, specs = [pr, prev], [row(PR_W), row(PR_W)]
    if has_vres:
        args.append(vfirst)
        specs.append(row(HW))
    args.append(state)
    specs.append(pl.BlockSpec((None,) + sblk, lambda b: (layer, b, 0, 0, 0)))
    for a in wts:
        args.append(a)
        specs.append(const(a))
    out_shape = [jax.ShapeDtypeStruct((bd_, 1, HW), BF16)]
    out_specs = [row(HW)]
    if not has_vres:
        out_shape.append(jax.ShapeDtypeStruct((bd_, 1, HW), F32))
        out_specs.append(row(HW))
    out_shape.append(jax.ShapeDtypeStruct((bd_, N_HEADS, HEAD_DIM, HEAD_DIM), F32))
    out_specs.append(pl.BlockSpec(sblk, lambda b: (b, 0, 0, 0)))
    outs = pl.pallas_call(
        functools.partial(_rwkv_dec_kernel, has_vres=has_vres),
        grid=(bd_,),
        in_specs=specs,
        out_specs=out_specs,
        out_shape=out_shape,
        scratch_shapes=[pltpu.VMEM((1, HW), F32)],
        compiler_params=_cparams("parallel"),
        name="rwkv_dec",
    )(*args)
    if has_vres:
        return outs[0], vfirst, outs[1]
    return outs[0], outs[1], outs[2]


def _mix_kernel(att_ref, rw_ref, x_ref, wo_ref, g_ref, b_ref, out_ref):
    d = functools.partial(jnp.dot, preferred_element_type=F32)
    mix = d(att_ref[...], wo_ref[0:HW, :]) + d(rw_ref[...], wo_ref[HW:2 * HW, :])
    out_ref[...] = _layer_norm(ALPHA * x_ref[...] + mix, g_ref[...], b_ref[...])


def _mix(att, rw, x, w_out, layer, g, b, tm):
    n = x.shape[0]
    row = lambda w: pl.BlockSpec((tm, w), lambda i: (i, 0))
    vec = pl.BlockSpec((None, 1, D_MODEL), lambda i: (layer, 0, 0))
    return pl.pallas_call(
        _mix_kernel,
        grid=(n // tm,),
        in_specs=[row(HW), row(HW), row(D_MODEL),
                  pl.BlockSpec((None, D_MODEL, D_MODEL), lambda i: (layer, 0, 0)), vec, vec],
        out_specs=row(D_MODEL),
        out_shape=jax.ShapeDtypeStruct((n, D_MODEL), F32),
        compiler_params=_cparams("parallel"),
        name="mix",
    )(att, rw, x, w_out, g, b)


FF_CHUNK = 1024


def _ffn_kernel(x_ref, wu_ref, wd_ref, g_ref, b_ref, out_ref):
    x = x_ref[...]
    xb = x.astype(BF16)
    acc = ALPHA * x
    for j in range(D_FF // FF_CHUNK):
        cs = slice(j * FF_CHUNK, (j + 1) * FF_CHUNK)
        h = jnp.maximum(jnp.dot(xb, wu_ref[:, cs], preferred_element_type=F32), 0.0)
        acc = acc + jnp.dot((h * h).astype(BF16), wd_ref[cs, :], preferred_element_type=F32)
    out_ref[...] = _layer_norm(acc, g_ref[...], b_ref[...])


def _ffn(x, w_up, w_down, layer, g, b, tm):
    n = x.shape[0]
    row = pl.BlockSpec((tm, D_MODEL), lambda i: (i, 0))
    vec = pl.BlockSpec((None, 1, D_MODEL), lambda i: (layer, 0, 0))
    return pl.pallas_call(
        _ffn_kernel,
        grid=(n // tm,),
        in_specs=[row, pl.BlockSpec((None, D_MODEL, D_FF), lambda i: (layer, 0, 0)),
                  pl.BlockSpec((None, D_FF, D_MODEL), lambda i: (layer, 0, 0)), vec, vec],
        out_specs=row,
        out_shape=jax.ShapeDtypeStruct((n, D_MODEL), F32),
        compiler_params=_cparams("parallel"),
        name="ffn",
    )(x, w_up, w_down, g, b)


def _rope_tables(pos):
    half = HEAD_DIM // 2
    inv = ROPE_THETA ** (-jnp.arange(half, dtype=F32) * (2.0 / HEAD_DIM))
    ang = pos.astype(F32)[:, None] * inv[None, :]
    cos, sin = jnp.cos(ang), jnp.sin(ang)
    cos2 = jnp.concatenate([cos, cos, cos, cos], axis=1)
    sin2 = jnp.concatenate([-sin, sin, -sin, sin], axis=1)
    return cos2, sin2


def kernel(x_prompt, x_sample, state_shift, state_wkv, cache_k, cache_v, w_in, w_vres_in, shift_mu, vres_mu, decay_base, decay_up, iclr_base, iclr_up, gate_up, vres_base, vres_up, key_scale_k, key_scale_a, bonus_rk, gn_g, gn_b, att_gain, w_out, ln1_g, ln1_b, w_ff_up, w_ff_down, ln2_g, ln2_b):
    B, T, _ = x_prompt.shape
    bd_, ds_, _ = x_sample.shape
    L = cache_k.shape[2]
    past_len = 8192
    assert ds_ == 1 and T % (DILATIONS[-1] * BLK) == 0 and L == DILATIONS[-1] * N_BACK
    keep = min(DILATIONS[-1] * N_BACK, T)
    tm_p, tb = 512, 256

    z32 = jnp.zeros((1, D_MODEL, LORA_VRES), F32)
    w_ext = jnp.concatenate([w_in, jnp.concatenate([z32, w_vres_in], 0),
                             jnp.zeros((DEPTH, D_MODEL, LO_W - 288), F32)], axis=-1).astype(BF16)
    mu_ext = jnp.concatenate([shift_mu, jnp.concatenate([jnp.zeros((1, LORA_VRES), F32), vres_mu], 0),
                              jnp.zeros((DEPTH, LO_W - 288), F32)], axis=-1)
    zdu = jnp.zeros((DEPTH, LORA_DECAY, HW), F32)
    up1 = jnp.concatenate([jnp.concatenate([decay_up, zdu], -1), jnp.concatenate([zdu, iclr_up], -1)],
                          axis=1).astype(BF16)
    gup = gate_up.astype(BF16)
    vup = jnp.concatenate([jnp.concatenate([jnp.zeros((1, LORA_VRES, HW), F32), vres_up], 0),
                           jnp.zeros((DEPTH, 128 - LORA_VRES, HW), F32)], axis=1).astype(BF16)
    vbase = jnp.concatenate([jnp.zeros((1, HW), F32), vres_base], 0)
    hid = jnp.arange(HW) // HEAD_DIM
    bd = (hid[:, None] == hid[None, :]).astype(BF16)
    w_out_b = w_out.astype(BF16)
    w_up_b = w_ff_up.astype(BF16)
    w_down_b = w_ff_down.astype(BF16)
    r3 = lambda a: a.reshape(DEPTH, 1, a.shape[-1])
    ln1g, ln1b, ln2g, ln2b = r3(ln1_g), r3(ln1_b), r3(ln2_g), r3(ln2_b)

    cos_p, sin_p = _rope_tables(jnp.arange(T))
    cos_s, sin_s = _rope_tables(jnp.full((bd_,), past_len))
    ck = cache_k.reshape(DEPTH, bd_, L, HW)
    cv = cache_v.reshape(DEPTH, bd_, L, HW)

    hp = x_prompt.reshape(B * T, D_MODEL)
    hs = x_sample.reshape(bd_, D_MODEL)
    vf_p = vf_s = None
    shift_p, shift_s, wkv_p, wkv_s, kp_l, vp_l, ks_l, vs_l = [], [], [], [], [], [], [], []
    for l in range(DEPTH):
        row1 = lambda a: a[l].reshape(1, -1)
        wts = (row1(mu_ext), row1(decay_base), up1[l], row1(iclr_base), gup[l], row1(vbase), vup[l],
               row1(key_scale_k), row1(key_scale_a), row1(bonus_rk), row1(gn_g), row1(gn_b), bd)
        gain = row1(att_gain)
        shift_p.append(hp.reshape(B, T, D_MODEL)[:, -1])
        shift_s.append(hs)

        q, kb, vb, kf, vf, pr = _proj(hp, w_ext, l, cos_p, sin_p, tm_p)
        r3d = lambda a: a.reshape(B, T, a.shape[-1])
        att = _att_prompt(r3d(q), r3d(kb), r3d(vb), gain)
        rw, vf_p, s_p = _rwkv_prompt(r3d(pr), vf_p, wts, tb)
        x1 = _mix(att.reshape(B * T, HW), rw.reshape(B * T, HW), hp, w_out_b, l, ln1g, ln1b, tm_p)
        hp = _ffn(x1, w_up_b, w_down_b, l, ln2g, ln2b, 256)
        wkv_p.append(s_p)
        kp_l.append(r3d(kf)[:, T - keep:].reshape(B, keep, N_HEADS, HEAD_DIM))
        vp_l.append(r3d(vf)[:, T - keep:].reshape(B, keep, N_HEADS, HEAD_DIM))

        qs, kbs, vbs, kfs, vfs, prs = _proj(hs, w_ext, l, cos_s, sin_s, bd_)
        prev = _proj_rwkv(state_shift[l], w_ext, l)
        e1 = lambda a: a.reshape(bd_, 1, a.shape[-1])
        att_s = _att_decode(e1(qs), e1(kbs), e1(vbs), ck, cv, l, gain)
        rw_s, vf_s, s_s = _rwkv_decode(e1(prs), e1(prev), vf_s, state_wkv, l, wts)
        x1s = _mix(att_s.reshape(bd_, HW), rw_s.reshape(bd_, HW), hs, w_out_b, l, ln1g, ln1b, bd_)
        hs = _ffn(x1s, w_up_b, w_down_b, l, ln2g, ln2b, bd_)
        wkv_s.append(s_s)
        ks_l.append(kfs.reshape(bd_, 1, N_HEADS, HEAD_DIM))
        vs_l.append(vfs.reshape(bd_, 1, N_HEADS, HEAD_DIM))

    return (hp.reshape(B, T, D_MODEL), hs.reshape(bd_, 1, D_MODEL), jnp.stack(shift_p), jnp.stack(shift_s),
            jnp.stack(wkv_p), jnp.stack(wkv_s), jnp.stack(kp_l), jnp.stack(vp_l), jnp.stack(ks_l),
            jnp.stack(vs_l))
```

```python
import functools

import jax
import jax.numpy as jnp
from jax import lax
from jax.experimental import pallas as pl
from jax.experimental.pallas import tpu as pltpu

F32 = jnp.float32
BF16 = jnp.bfloat16

D_MODEL = 1024
DEPTH = 4
HEAD_DIM = 64
N_HEADS = 8
HW = N_HEADS * HEAD_DIM
LORA_DECAY, LORA_ICLR, LORA_GATE, LORA_VRES = 64, 64, 128, 32
LO_W = 384
PR_W = 3 * HW + LO_W
P_EXT = 3 * HW + PR_W
DILATIONS = (1, 4, 16)
N_BACK = 128
BLK = 128
ST_W = HW + 256
ROPE_THETA = 10000.0
D_FF = 4 * D_MODEL
ALPHA = (2.0 * DEPTH) ** 0.25
LN_EPS = 1e-5
GN_EPS = 64e-5
RMS_EPS = 1e-6
CHUNK = 64
VMEM_LIMIT = 56 * 1024 * 1024


def _cparams(*sem):
    return pltpu.CompilerParams(dimension_semantics=sem, vmem_limit_bytes=VMEM_LIMIT)


def _mm(a, b):
    return jnp.dot(a.astype(BF16), b.astype(BF16), preferred_element_type=F32)


def _split2(x):
    hi = x.astype(BF16)
    lo = (x - hi.astype(F32)).astype(BF16)
    return hi, lo


def _sum_heads(x, bd):
    hi, lo = _split2(x)
    d = functools.partial(jnp.dot, preferred_element_type=F32)
    return d(hi, bd) + d(lo, bd)


def _layer_norm(x, g, b):
    mu = jnp.mean(x, axis=-1, keepdims=True)
    xc = x - mu
    var = jnp.mean(xc * xc, axis=-1, keepdims=True)
    return xc * lax.rsqrt(var + LN_EPS) * g + b


def _rope(p, cos, sin):
    lane = lax.broadcasted_iota(jnp.int32, p.shape, 1) % HEAD_DIM
    rot = jnp.where(lane < HEAD_DIM // 2, pltpu.roll(p, HW - HEAD_DIM // 2, 1),
                    pltpu.roll(p, HEAD_DIM // 2, 1))
    return p * cos + rot * sin


def _proj_kernel(x_ref, w_ref, cos_ref, sin_ref, q_ref, kb_ref, vb_ref, kf_ref, vf_ref, pr_ref):
    xb = x_ref[...].astype(BF16)
    cos = jnp.concatenate([cos_ref[...]] * 4, axis=1)
    sin = jnp.concatenate([sin_ref[...]] * 4, axis=1)
    d = functools.partial(jnp.dot, preferred_element_type=F32)
    q = _rope(d(xb, w_ref[:, 0:HW]), cos, sin) * (HEAD_DIM ** -0.5)
    q_ref[...] = q.astype(BF16)
    k = _rope(d(xb, w_ref[:, HW:2 * HW]), cos, sin)
    kf_ref[...] = k
    kb_ref[...] = k.astype(BF16)
    v = d(xb, w_ref[:, 2 * HW:3 * HW])
    vf_ref[...] = v
    vb_ref[...] = v.astype(BF16)
    pr_ref[...] = d(xb, w_ref[:, 3 * HW:P_EXT])


def _proj(x, w_ext, layer, cos, sin, tm):
    n = x.shape[0]
    t_tiles = cos.shape[0] // tm
    row = lambda i: (i, 0)
    return pl.pallas_call(
        _proj_kernel,
        grid=(n // tm,),
        in_specs=[pl.BlockSpec((tm, D_MODEL), row),
                  pl.BlockSpec((None, D_MODEL, P_EXT), lambda i: (layer, 0, 0)),
                  pl.BlockSpec((tm, 128), lambda i: (i % t_tiles, 0)),
                  pl.BlockSpec((tm, 128), lambda i: (i % t_tiles, 0))],
        out_specs=[pl.BlockSpec((tm, HW), row)] * 5 + [pl.BlockSpec((tm, PR_W), row)],
        out_shape=[jax.ShapeDtypeStruct((n, HW), BF16)] * 3 + [jax.ShapeDtypeStruct((n, HW), F32)] * 2
        + [jax.ShapeDtypeStruct((n, PR_W), F32)],
        compiler_params=_cparams("parallel"),
        name="proj",
    )(x, w_ext, cos, sin)


def _proj_rwkv_kernel(x_ref, w_ref, pr_ref):
    pr_ref[...] = jnp.dot(x_ref[...].astype(BF16), w_ref[:, 3 * HW:P_EXT], preferred_element_type=F32)


def _proj_rwkv(x, w_ext, layer):
    n = x.shape[0]
    return pl.pallas_call(
        _proj_rwkv_kernel,
        grid=(1,),
        in_specs=[pl.BlockSpec((n, D_MODEL), lambda i: (0, 0)),
                  pl.BlockSpec((None, D_MODEL, P_EXT), lambda i: (layer, 0, 0))],
        out_specs=pl.BlockSpec((n, PR_W), lambda i: (0, 0)),
        out_shape=jax.ShapeDtypeStruct((n, PR_W), F32),
        compiler_params=_cparams("arbitrary"),
        name="proj_prev",
    )(x, w_ext)


def _att_kernel(*refs, has_prev, is_last):
    q_ref, k_ref, v_ref, kp_ref, vp_ref = refs[:5]
    pos = 5
    st_ref = gain_ref = None
    if has_prev:
        st_ref = refs[pos]
        pos += 1
    if is_last:
        gain_ref = refs[pos]
        pos += 1
    out_ref = refs[pos]

    n = pl.program_id(2)
    q = q_ref[...]
    kc = k_ref[...]
    vc = v_ref[...]
    kp = kp_ref[...]
    vp = vp_ref[...]
    qi = lax.broadcasted_iota(jnp.int32, (BLK, BLK), 0)
    ki = lax.broadcasted_iota(jnp.int32, (BLK, BLK), 1)
    valid_c = ki <= qi
    valid_p = jnp.where(ki >= qi, jnp.where(n > 0, 1, 0), 0) > 0
    nt = (((1,), (1,)), ((), ()))
    sls = [slice(h * HEAD_DIM, (h + 1) * HEAD_DIM) for h in range(N_HEADS)]
    s_c = [jnp.where(valid_c, lax.dot_general(q[:, sl], kc[:, sl], nt, preferred_element_type=F32), -jnp.inf)
           for sl in sls]
    s_p = [jnp.where(valid_p, lax.dot_general(q[:, sl], kp[:, sl], nt, preferred_element_type=F32), -jnp.inf)
           for sl in sls]
    m_h = [jnp.maximum(jnp.max(a, axis=-1, keepdims=True), jnp.max(b, axis=-1, keepdims=True))
           for a, b in zip(s_c, s_p)]
    e_c = [jnp.exp(a - mm) for a, mm in zip(s_c, m_h)]
    e_p = [jnp.exp(a - mm) for a, mm in zip(s_p, m_h)]
    l_h = [jnp.sum(a, axis=-1, keepdims=True) + jnp.sum(b, axis=-1, keepdims=True) for a, b in zip(e_c, e_p)]
    acc = jnp.concatenate([jnp.dot(a.astype(BF16), vc[:, sl], preferred_element_type=F32)
                           + jnp.dot(b.astype(BF16), vp[:, sl], preferred_element_type=F32)
                           for a, b, sl in zip(e_c, e_p, sls)], axis=1)
    lane = lax.broadcasted_iota(jnp.int32, (BLK, 128), 1)
    m = jnp.zeros((BLK, 128), F32)
    l = jnp.ones((BLK, 128), F32)
    for h in range(N_HEADS):
        m = jnp.where(lane == h, m_h[h], m)
        l = jnp.where(lane == h, l_h[h], l)
    head_of_lane = lax.broadcasted_iota(jnp.int32, (128, HW), 1) // HEAD_DIM
    sel = jnp.where(lax.broadcasted_iota(jnp.int32, (128, HW), 0) == head_of_lane, 1.0, 0.0).astype(BF16)

    def expand(x):
        hi = x.astype(BF16)
        r1 = x - hi.astype(F32)
        mid = r1.astype(BF16)
        low = (r1 - mid.astype(F32)).astype(BF16)
        d = functools.partial(jnp.dot, preferred_element_type=F32)
        return d(hi, sel) + d(mid, sel) + d(low, sel)

    if has_prev:
        m0 = st_ref[:, HW:HW + 128]
        l0 = st_ref[:, HW + 128:ST_W]
        mn = jnp.maximum(m0, m)
        a0 = jnp.exp(m0 - mn)
        a1 = jnp.exp(m - mn)
        acc = expand(a0) * st_ref[:, 0:HW] + expand(a1) * acc
        l = a0 * l0 + a1 * l
        m = mn
    if is_last:
        att = acc * expand(1.0 / l)
        ms = jnp.mean(att * att, axis=-1, keepdims=True)
        out_ref[...] = (att * lax.rsqrt(ms + RMS_EPS) * gain_ref[...]).astype(BF16)
    else:
        out_ref[:, 0:HW] = acc
        out_ref[:, HW:HW + 128] = m
        out_ref[:, HW + 128:ST_W] = l


def _att_pattern(q, kb, vb, state, gain, d, is_last):
    B, T, _ = q.shape
    M = T // d
    nb = M // BLK
    view = lambda a, w: a.reshape(B, M, d * w)
    blk = lambda w: pl.BlockSpec((None, BLK, w), lambda b, r, n: (b, n, r))
    has_prev = state is not None
    pblk = pl.BlockSpec((None, BLK, HW), lambda b, r, n: (b, jnp.maximum(n - 1, 0), r))
    args = [view(q, HW), view(kb, HW), view(vb, HW), view(kb, HW), view(vb, HW)]
    in_specs = [blk(HW)] * 3 + [pblk] * 2
    if has_prev:
        args.append(view(state, ST_W))
        in_specs.append(blk(ST_W))
    if is_last:
        args.append(gain)
        in_specs.append(pl.BlockSpec((1, HW), lambda b, r, n: (0, 0)))
    out_w, out_dt = (HW, BF16) if is_last else (ST_W, F32)
    out = pl.pallas_call(
        functools.partial(_att_kernel, has_prev=has_prev, is_last=is_last),
        grid=(B, d, nb),
        in_specs=in_specs,
        out_specs=blk(out_w),
        out_shape=jax.ShapeDtypeStruct((B, M, d * out_w), out_dt),
        compiler_params=_cparams("parallel", "parallel", "arbitrary"),
        name=f"att_d{d}",
    )(*args)
    return out.reshape(B, T, out_w)


def _att_prompt(q, kb, vb, gain):
    state = None
    for i, d in enumerate(DILATIONS):
        state = _att_pattern(q, kb, vb, state, gain, d, i == len(DILATIONS) - 1)
    return state


def _att_dec_kernel(q_ref, kn_ref, vn_ref, k1, k4, k16, v1, v4, v16, gain_ref, out_ref):
    q = q_ref[...]
    kn = kn_ref[...]
    vn = vn_ref[...]
    s_new = jnp.sum(q * kn, axis=-1, keepdims=True)
    scores = [jnp.sum(kr[...] * q[None], axis=-1, keepdims=True) for kr in (k1, k4, k16)]
    m = s_new
    for s in scores:
        m = jnp.maximum(m, jnp.max(s, axis=0))
    e_new = jnp.exp(s_new - m) * float(len(DILATIONS))
    l = e_new
    acc = e_new * vn
    for s, vr in zip(scores, (v1, v4, v16)):
        e = jnp.exp(s - m[None])
        l = l + jnp.sum(e, axis=0)
        acc = acc + jnp.sum(e * vr[...], axis=0)
    o = acc / l
    ms = jnp.sum(jnp.sum(o * o, axis=-1, keepdims=True), axis=0, keepdims=True) * (1.0 / HW)
    out_ref[...] = o * lax.rsqrt(ms + RMS_EPS) * gain_ref[...]


def _att_decode(q, kn, vn, cache_k, cache_v, layer, gain):
    bd, L = cache_k.shape[1], cache_k.shape[2]
    row = pl.BlockSpec((None, N_HEADS, HEAD_DIM), lambda b: (b, 0, 0))
    args, specs = [q, kn, vn], [row, row, row]
    for c in (cache_k, cache_v):
        for d in DILATIONS:
            assert (L // d) % BLK == 0
            args.append(c.reshape(DEPTH, bd, L // d, d, N_HEADS, HEAD_DIM))
            last = L // d // BLK - 1
            specs.append(pl.BlockSpec((None, None, BLK, None, N_HEADS, HEAD_DIM),
                                      lambda b, last=last: (layer, b, last, 0, 0, 0)))
    args.append(gain.reshape(N_HEADS, HEAD_DIM))
    specs.append(pl.BlockSpec((N_HEADS, HEAD_DIM), lambda b: (0, 0)))
    return pl.pallas_call(
        _att_dec_kernel,
        grid=(bd,),
        in_specs=specs,
        out_specs=row,
        out_shape=jax.ShapeDtypeStruct((bd, N_HEADS, HEAD_DIM), F32),
        compiler_params=_cparams("parallel"),
        name="att_dec",
    )(*args)


def _softplus(x):
    return jnp.maximum(x, 0.0) + jnp.log(1.0 + jnp.exp(-jnp.abs(x)))


def _rwkv_pre(z, vfirst, prm, has_vres):
    (dbase, up1, ibase, gup, vbase, vup, ksk, ksa, bd) = prm
    r = z[:, 0:HW]
    k = z[:, HW:2 * HW]
    v = z[:, 2 * HW:3 * HW]
    lo1 = z[:, 3 * HW:3 * HW + 128]
    lane = lax.broadcasted_iota(jnp.int32, lo1.shape, 1)
    lo1 = jnp.where(lane < LORA_DECAY, jnp.tanh(lo1), lo1)
    up = _mm(lo1, up1)
    w = -_softplus(-(dbase + up[:, 0:HW])) - 0.5
    logw = -jnp.exp(w)
    iclr = jax.nn.sigmoid(ibase + up[:, HW:2 * HW])
    g = _mm(jax.nn.sigmoid(z[:, 3 * HW + 128:3 * HW + 256]), gup)
    if has_vres:
        mix = jax.nn.sigmoid(vbase + _mm(z[:, 3 * HW + 256:3 * HW + 384], vup))
        v = v + (vfirst - v) * mix
    kk = k * ksk
    nrm = jnp.sqrt(_sum_heads(kk * kk, bd))
    kk = kk / jnp.maximum(nrm, 1e-12)
    k = k * (1.0 + (iclr - 1.0) * ksa)
    return r, k, v, logw, -kk, kk * iclr, g


def _rwkv_post(y, r, k, v, g, brk, gng, gnb, bd):
    mu = _sum_heads(y, bd) * (1.0 / HEAD_DIM)
    yc = y - mu
    var = _sum_heads(yc * yc, bd) * (1.0 / HEAD_DIM)
    yn = yc * lax.rsqrt(var + GN_EPS) * gng + gnb
    bonus = _sum_heads(r * k * brk, bd) * v
    return (yn + bonus) * g


def _bmm(a, b):
    return jnp.einsum('bij,bjk->bik', a.astype(BF16), b.astype(BF16), preferred_element_type=F32)


def _bmm_hi(a, b):
    ah, al = _split2(a)
    bh, bl = _split2(b)
    d = functools.partial(jnp.einsum, 'bij,bjk->bik', preferred_element_type=F32)
    return d(ah, bh) + d(ah, bl) + d(al, bh)


def _rwkv_chunk(at, rt, btT, ktT, vv, pcb, h0, masks):
    strict, incl, eye, levels = masks
    c = CHUNK
    qr = jnp.concatenate([at, rt], axis=1)
    gb = _bmm(qr, btT)
    gk = _bmm(qr, ktT)
    a_ab = jnp.where(strict, gb[:, 0:c], 0.0)
    a_rb = jnp.where(incl, gb[:, c:2 * c], 0.0)
    a_ak = jnp.where(strict, gk[:, 0:c], 0.0)
    a_rk = jnp.where(incl, gk[:, c:2 * c], 0.0)
    t = eye + jnp.where(levels[0], a_ab, 0.0)
    for lv in levels[1:]:
        t = t + _bmm(_bmm(t, jnp.where(lv, a_ab, 0.0)), t)
    xv = _bmm(jnp.concatenate([a_ak, a_rk, ktT], axis=1), vv)
    w = _bmm(t, at)
    z = _bmm(t, xv[:, 0:c])
    lhs = jnp.concatenate([a_rb, btT], axis=1)
    lw = _bmm(lhs, w)
    lz = _bmm(lhs, z)
    rp = rt + lw[:, 0:c]
    y0 = lz[:, 0:c] + xv[:, c:2 * c]
    m = pcb * (eye + lw[:, c:2 * c])
    n = pcb * (lz[:, c:2 * c] + xv[:, 2 * c:3 * c])
    y = _bmm(rp, h0) + y0
    h1 = _bmm_hi(m, h0) + n
    return y, h1


def _chunk_masks():
    ri = lax.broadcasted_iota(jnp.int32, (CHUNK, CHUNK), 0)
    ci = lax.broadcasted_iota(jnp.int32, (CHUNK, CHUNK), 1)
    strict = ri > ci
    incl = ri >= ci
    eye = jnp.where(ri == ci, 1.0, 0.0).astype(F32)
    levels = []
    size = 2
    while size <= CHUNK:
        same = (ri // size) == (ci // size)
        if size == 2:
            levels.append(same)
        else:
            inner = (ri // (size // 2)) == (ci // (size // 2))
            levels.append(jnp.where(same, jnp.where(inner, 0, 1), 0) > 0)
        size *= 2
    return strict, incl, eye, levels


def _rwkv_kernel(*refs, has_vres, tb):
    pr_ref = refs[0]
    pos = 1
    vf_ref = None
    if has_vres:
        vf_ref = refs[pos]
        pos += 1
    (mu_ref, dbase, up1, ibase, gup, vbase, vup, ksk, ksa, brk, gng, gnb, bd_ref, tri_ref) = refs[pos:pos + 14]
    pos += 14
    rw_ref = refs[pos]
    pos += 1
    vfo_ref = None
    if not has_vres:
        vfo_ref = refs[pos]
        pos += 1
    so_ref = refs[pos]
    pos += 1
    h_scr, carry, at4, rt4, bt4, kt4, v4, pc4, y4, y_s = refs[pos:pos + 10]
    nc = tb // CHUNK

    i = pl.program_id(1)

    @pl.when(i == 0)
    def _():
        h_scr[...] = jnp.zeros_like(h_scr)
        carry[...] = jnp.zeros_like(carry)

    p = pr_ref[...]
    row = lax.broadcasted_iota(jnp.int32, p.shape, 0)
    prev = jnp.where(row == 0, carry[0:1, :], pltpu.roll(p, 1, 0))
    carry[0:1, :] = p[tb - 1:tb, :]
    z = p + (prev - p) * mu_ref[...]
    bd = bd_ref[...]
    prm = (dbase[...], up1[...], ibase[...], gup[...], vbase[...], vup[...], ksk[...], ksa[...], bd)
    r, k, v, logw, a, b, g = _rwkv_pre(z, vf_ref[...] if has_vres else None, prm, has_vres)
    if not has_vres:
        vfo_ref[...] = v
    hi = logw.astype(BF16)
    r1 = logw - hi.astype(F32)
    mid = r1.astype(BF16)
    low = (r1 - mid.astype(F32)).astype(BF16)
    tri = tri_ref[...]
    d = functools.partial(jnp.dot, preferred_element_type=F32)
    cum = d(tri, hi) + d(tri, mid) + d(tri, low)
    ecum = jnp.exp(cum)
    einv = jnp.exp(-cum)
    a_t = a * jnp.exp(cum - logw)
    r_t = r * ecum
    btT = (b * einv).T
    ktT = (k * einv).T
    pcT = ecum.T
    for c in range(nc):
        rs = slice(c * CHUNK, (c + 1) * CHUNK)
        for h in range(N_HEADS):
            ls = slice(h * HEAD_DIM, (h + 1) * HEAD_DIM)
            at4[c, h] = a_t[rs, ls]
            rt4[c, h] = r_t[rs, ls]
            v4[c, h] = v[rs, ls]
            bt4[c, h] = btT[ls, rs]
            kt4[c, h] = ktT[ls, rs]
            pc4[c, h] = jnp.broadcast_to(pcT[ls, (c + 1) * CHUNK - 1:(c + 1) * CHUNK], (HEAD_DIM, CHUNK))
    masks = _chunk_masks()

    def chunk(c, _):
        y, h1 = _rwkv_chunk(at4[c], rt4[c], bt4[c], kt4[c], v4[c], pc4[c], h_scr[...], masks)
        y4[c] = y
        h_scr[...] = h1
        return 0

    lax.fori_loop(0, nc, chunk, 0)
    for c in range(nc):
        for h in range(N_HEADS):
            y_s[c * CHUNK:(c + 1) * CHUNK, h * HEAD_DIM:(h + 1) * HEAD_DIM] = y4[c, h]
    rw = _rwkv_post(y_s[...], r, k, v, g, brk[...], gng[...], gnb[...], bd)
    rw_ref[...] = rw.astype(BF16)

    @pl.when(i == pl.num_programs(1) - 1)
    def _():
        hst = h_scr[...]
        hi = hst.astype(BF16)
        r1 = hst - hi.astype(F32)
        mid = r1.astype(BF16)
        low = (r1 - mid.astype(F32)).astype(BF16)
        eye = jnp.broadcast_to(masks[2].astype(BF16)[None], (N_HEADS, HEAD_DIM, HEAD_DIM))
        d = functools.partial(jnp.einsum, 'bik,bjk->bij', preferred_element_type=F32)
        so_ref[...] = d(eye, hi) + d(eye, mid) + d(eye, low)


def _rwkv_prompt(pr, vfirst, wts, tb):
    B, T, _ = pr.shape
    has_vres = vfirst is not None
    blk = lambda w: pl.BlockSpec((None, tb, w), lambda b, i: (b, i, 0))
    const = lambda a: pl.BlockSpec(a.shape, lambda b, i: (0,) * a.ndim)
    args, specs = [pr], [blk(PR_W)]
    if has_vres:
        args.append(vfirst)
        specs.append(blk(HW))
    idx = jnp.arange(tb)
    tri = ((idx[:, None] // CHUNK == idx[None, :] // CHUNK) & (idx[:, None] >= idx[None, :])).astype(BF16)
    for a in wts + (tri,):
        args.append(a)
        specs.append(const(a))
    out_shape = [jax.ShapeDtypeStruct((B, T, HW), BF16)]
    out_specs = [blk(HW)]
    if not has_vres:
        out_shape.append(jax.ShapeDtypeStruct((B, T, HW), F32))
        out_specs.append(blk(HW))
    out_shape.append(jax.ShapeDtypeStruct((B, N_HEADS, HEAD_DIM, HEAD_DIM), F32))
    out_specs.append(pl.BlockSpec((None, N_HEADS, HEAD_DIM, HEAD_DIM), lambda b, i: (b, 0, 0, 0)))
    per_head = lambda: pltpu.VMEM((tb // CHUNK, N_HEADS, HEAD_DIM, HEAD_DIM), F32)
    scratch = [pltpu.VMEM((N_HEADS, HEAD_DIM, HEAD_DIM), F32), pltpu.VMEM((8, PR_W), F32)]
    scratch += [per_head() for _ in range(7)] + [pltpu.VMEM((tb, HW), F32)]
    outs = pl.pallas_call(
        functools.partial(_rwkv_kernel, has_vres=has_vres, tb=tb),
        grid=(B, T // tb),
        in_specs=specs,
        out_specs=out_specs,
        out_shape=out_shape,
        scratch_shapes=scratch,
        compiler_params=_cparams("parallel", "arbitrary"),
        name="rwkv",
    )(*args)
    if has_vres:
        return outs[0], vfirst, outs[1]
    return outs[0], outs[1], outs[2]


def _rwkv_dec_kernel(*refs, has_vres):
    pr_ref, pv_ref = refs[0:2]
    pos = 2
    vf_ref = None
    if has_vres:
        vf_ref = refs[pos]
        pos += 1
    s_ref = refs[pos]
    pos += 1
    (mu_ref, dbase, up1, ibase, gup, vbase, vup, ksk, ksa, brk, gng, gnb, bd_ref) = refs[pos:pos + 13]
    pos += 13
    rw_ref = refs[pos]
    pos += 1
    vfo_ref = None
    if not has_vres:
        vfo_ref = refs[pos]
        pos += 1
    so_ref, y_s = refs[pos:pos + 2]

    p = pr_ref[...]
    z = p + (pv_ref[...] - p) * mu_ref[...]
    bd = bd_ref[...]
    prm = (dbase[...], up1[...], ibase[...], gup[...], vbase[...], vup[...], ksk[...], ksa[...], bd)
    r, k, v, logw, a, b, g = _rwkv_pre(z, vf_ref[...] if has_vres else None, prm, has_vres)
    if not has_vres:
        vfo_ref[...] = v
    w = jnp.exp(logw)
    ri = lax.broadcasted_iota(jnp.int32, (HEAD_DIM, HEAD_DIM), 0)
    ci = lax.broadcasted_iota(jnp.int32, (HEAD_DIM, HEAD_DIM), 1)
    eye = jnp.where(ri == ci, 1.0, 0.0).astype(F32)
    for h in range(N_HEADS):
        sl = slice(h * HEAD_DIM, (h + 1) * HEAD_DIM)
        s = s_ref[h]
        sa = jnp.sum(s * a[:, sl], axis=-1, keepdims=True)
        vcol = jnp.sum(eye * v[:, sl], axis=-1, keepdims=True)
        s = s * w[:, sl] + sa * b[:, sl] + vcol * k[:, sl]
        so_ref[h] = s
        ycol = jnp.sum(s * r[:, sl], axis=-1, keepdims=True)
        y_s[:, sl] = jnp.sum(eye * ycol, axis=0, keepdims=True)
    rw = _rwkv_post(y_s[...], r, k, v, g, brk[...], gng[...], gnb[...], bd)
    rw_ref[...] = rw.astype(BF16)


def _rwkv_decode(pr, prev, vfirst, state, layer, wts):
    bd_ = pr.shape[0]
    has_vres = vfirst is not None
    row = lambda w: pl.BlockSpec((None, 1, w), lambda b: (b, 0, 0))
    const = lambda a: pl.BlockSpec(a.shape, lambda b: (0,) * a.ndim)
    sblk = (None, N_HEADS, HEAD_DIM, HEAD_DIM)
    args, specs = [pr, prev], [row(PR_W), row(PR_W)]
    if has_vres:
        args.append(vfirst)
        specs.append(row(HW))
    args.append(state)
    specs.append(pl.BlockSpec((None,) + sblk, lambda b: (layer, b, 0, 0, 0)))
    for a in wts:
        args.append(a)
        specs.append(const(a))
    out_shape = [jax.ShapeDtypeStruct((bd_, 1, HW), BF16)]
    out_specs = [row(HW)]
    if not has_vres:
        out_shape.append(jax.ShapeDtypeStruct((bd_, 1, HW), F32))
        out_specs.append(row(HW))
    out_shape.append(jax.ShapeDtypeStruct((bd_, N_HEADS, HEAD_DIM, HEAD_DIM), F32))
    out_specs.append(pl.BlockSpec(sblk, lambda b: (b, 0, 0, 0)))
    outs = pl.pallas_call(
        functools.partial(_rwkv_dec_kernel, has_vres=has_vres),
        grid=(bd_,),
        in_specs=specs,
        out_specs=out_specs,
        out_shape=out_shape,
        scratch_shapes=[pltpu.VMEM((1, HW), F32)],
        compiler_params=_cparams("parallel"),
        name="rwkv_dec",
    )(*args)
    if has_vres:
        return outs[0], vfirst, outs[1]
    return outs[0], outs[1], outs[2]


def _mix_kernel(att_ref, rw_ref, x_ref, wo_ref, g_ref, b_ref, out_ref):
    d = functools.partial(jnp.dot, preferred_element_type=F32)
    mix = d(att_ref[...], wo_ref[0:HW, :]) + d(rw_ref[...], wo_ref[HW:2 * HW, :])
    out_ref[...] = _layer_norm(ALPHA * x_ref[...] + mix, g_ref[...], b_ref[...])


def _mix(att, rw, x, w_out, layer, g, b, tm):
    n = x.shape[0]
    row = lambda w: pl.BlockSpec((tm, w), lambda i: (i, 0))
    vec = pl.BlockSpec((None, 1, D_MODEL), lambda i: (layer, 0, 0))
    return pl.pallas_call(
        _mix_kernel,
        grid=(n // tm,),
        in_specs=[row(HW), row(HW), row(D_MODEL),
                  pl.BlockSpec((None, D_MODEL, D_MODEL), lambda i: (layer, 0, 0)), vec, vec],
        out_specs=row(D_MODEL),
        out_shape=jax.ShapeDtypeStruct((n, D_MODEL), F32),
        compiler_params=_cparams("parallel"),
        name="mix",
    )(att, rw, x, w_out, g, b)


FF_CHUNK = 1024


def _ffn_kernel(x_ref, wu_ref, wd_ref, g_ref, b_ref, out_ref):
    x = x_ref[...]
    xb = x.astype(BF16)
    acc = ALPHA * x
    for j in range(D_FF // FF_CHUNK):
        cs = slice(j * FF_CHUNK, (j + 1) * FF_CHUNK)
        h = jnp.maximum(jnp.dot(xb, wu_ref[:, cs], preferred_element_type=F32), 0.0)
        acc = acc + jnp.dot((h * h).astype(BF16), wd_ref[cs, :], preferred_element_type=F32)
    out_ref[...] = _layer_norm(acc, g_ref[...], b_ref[...])


def _ffn(x, w_up, w_down, layer, g, b, tm):
    n = x.shape[0]
    row = pl.BlockSpec((tm, D_MODEL), lambda i: (i, 0))
    vec = pl.BlockSpec((None, 1, D_MODEL), lambda i: (layer, 0, 0))
    return pl.pallas_call(
        _ffn_kernel,
        grid=(n // tm,),
        in_specs=[row, pl.BlockSpec((None, D_MODEL, D_FF), lambda i: (layer, 0, 0)),
                  pl.BlockSpec((None, D_FF, D_MODEL), lambda i: (layer, 0, 0)), vec, vec],
        out_specs=row,
        out_shape=jax.ShapeDtypeStruct((n, D_MODEL), F32),
        compiler_params=_cparams("parallel"),
        name="ffn",
    )(x, w_up, w_down, g, b)


def _rope_tables(pos):
    half = HEAD_DIM // 2
    inv = ROPE_THETA ** (-jnp.arange(half, dtype=F32) * (2.0 / HEAD_DIM))
    ang = pos.astype(F32)[:, None] * inv[None, :]
    cos, sin = jnp.cos(ang), jnp.sin(ang)
    cos2 = jnp.concatenate([cos, cos, cos, cos], axis=1)
    sin2 = jnp.concatenate([-sin, sin, -sin, sin], axis=1)
    return cos2, sin2


def kernel(x_prompt, x_sample, state_shift, state_wkv, cache_k, cache_v, w_in, w_vres_in, shift_mu, vres_mu, decay_base, decay_up, iclr_base, iclr_up, gate_up, vres_base, vres_up, key_scale_k, key_scale_a, bonus_rk, gn_g, gn_b, att_gain, w_out, ln1_g, ln1_b, w_ff_up, w_ff_down, ln2_g, ln2_b):
    B, T, _ = x_prompt.shape
    bd_, ds_, _ = x_sample.shape
    L = cache_k.shape[2]
    past_len = 8192
    assert ds_ == 1 and T % (DILATIONS[-1] * BLK) == 0 and L == DILATIONS[-1] * N_BACK
    keep = min(DILATIONS[-1] * N_BACK, T)
    tm_p, tb = 512, 256

    z32 = jnp.zeros((1, D_MODEL, LORA_VRES), F32)
    w_ext = jnp.concatenate([w_in, jnp.concatenate([z32, w_vres_in], 0),
                             jnp.zeros((DEPTH, D_MODEL, LO_W - 288), F32)], axis=-1).astype(BF16)
    mu_ext = jnp.concatenate([shift_mu, jnp.concatenate([jnp.zeros((1, LORA_VRES), F32), vres_mu], 0),
                              jnp.zeros((DEPTH, LO_W - 288), F32)], axis=-1)
    zdu = jnp.zeros((DEPTH, LORA_DECAY, HW), F32)
    up1 = jnp.concatenate([jnp.concatenate([decay_up, zdu], -1), jnp.concatenate([zdu, iclr_up], -1)],
                          axis=1).astype(BF16)
    gup = gate_up.astype(BF16)
    vup = jnp.concatenate([jnp.concatenate([jnp.zeros((1, LORA_VRES, HW), F32), vres_up], 0),
                           jnp.zeros((DEPTH, 128 - LORA_VRES, HW), F32)], axis=1).astype(BF16)
    vbase = jnp.concatenate([jnp.zeros((1, HW), F32), vres_base], 0)
    hid = jnp.arange(HW) // HEAD_DIM
    bd = (hid[:, None] == hid[None, :]).astype(BF16)
    w_out_b = w_out.astype(BF16)
    w_up_b = w_ff_up.astype(BF16)
    w_down_b = w_ff_down.astype(BF16)
    r3 = lambda a: a.reshape(DEPTH, 1, a.shape[-1])
    ln1g, ln1b, ln2g, ln2b = r3(ln1_g), r3(ln1_b), r3(ln2_g), r3(ln2_b)

    cos_p, sin_p = _rope_tables(jnp.arange(T))
    cos_s, sin_s = _rope_tables(jnp.full((bd_,), past_len))

    hp = x_prompt.reshape(B * T, D_MODEL)
    hs = x_sample.reshape(bd_, D_MODEL)
    vf_p = vf_s = None
    shift_p, shift_s, wkv_p, wkv_s, kp_l, vp_l, ks_l, vs_l = [], [], [], [], [], [], [], []
    for l in range(DEPTH):
        row1 = lambda a: a[l].reshape(1, -1)
        wts = (row1(mu_ext), row1(decay_base), up1[l], row1(iclr_base), gup[l], row1(vbase), vup[l],
               row1(key_scale_k), row1(key_scale_a), row1(bonus_rk), row1(gn_g), row1(gn_b), bd)
        gain = row1(att_gain)
        shift_p.append(hp.reshape(B, T, D_MODEL)[:, -1])
        shift_s.append(hs)

        q, kb, vb, kf, vf, pr = _proj(hp, w_ext, l, cos_p, sin_p, tm_p)
        r3d = lambda a: a.reshape(B, T, a.shape[-1])
        att = _att_prompt(r3d(q), r3d(kb), r3d(vb), gain)
        rw, vf_p, s_p = _rwkv_prompt(r3d(pr), vf_p, wts, tb)
        x1 = _mix(att.reshape(B * T, HW), rw.reshape(B * T, HW), hp, w_out_b, l, ln1g, ln1b, tm_p)
        hp = _ffn(x1, w_up_b, w_down_b, l, ln2g, ln2b, 256)
        wkv_p.append(s_p)
        kp_l.append(r3d(kf)[:, T - keep:].reshape(B, keep, N_HEADS, HEAD_DIM))
        vp_l.append(r3d(vf)[:, T - keep:].reshape(B, keep, N_HEADS, HEAD_DIM))

        qs, kbs, vbs, kfs, vfs, prs = _proj(hs, w_ext, l, cos_s, sin_s, bd_)
        prev = _proj_rwkv(state_shift[l], w_ext, l)
        e1 = lambda a: a.reshape(bd_, 1, a.shape[-1])
        hd = lambda a: a.astype(F32).reshape(bd_, N_HEADS, HEAD_DIM)
        att_s = _att_decode(hd(qs), hd(kfs), hd(vfs), cache_k, cache_v, l, gain)
        rw_s, vf_s, s_s = _rwkv_decode(e1(prs), e1(prev), vf_s, state_wkv, l, wts)
        x1s = _mix(att_s.reshape(bd_, HW).astype(BF16), rw_s.reshape(bd_, HW), hs, w_out_b, l, ln1g, ln1b, bd_)
        hs = _ffn(x1s, w_up_b, w_down_b, l, ln2g, ln2b, bd_)
        wkv_s.append(s_s)
        ks_l.append(kfs.reshape(bd_, 1, N_HEADS, HEAD_DIM))
        vs_l.append(vfs.reshape(bd_, 1, N_HEADS, HEAD_DIM))

    return (hp.reshape(B, T, D_MODEL), hs.reshape(bd_, 1, D_MODEL), jnp.stack(shift_p), jnp.stack(shift_s),
            jnp.stack(wkv_p), jnp.stack(wkv_s), jnp.stack(kp_l), jnp.stack(vp_l), jnp.stack(ks_l),
            jnp.stack(vs_l))
```

```python
import functools

import jax
import jax.numpy as jnp
from jax import lax
from jax.experimental import pallas as pl
from jax.experimental.pallas import tpu as pltpu

F32 = jnp.float32
BF16 = jnp.bfloat16

D_MODEL = 1024
DEPTH = 4
HEAD_DIM = 64
N_HEADS = 8
HW = N_HEADS * HEAD_DIM
LORA_DECAY, LORA_ICLR, LORA_GATE, LORA_VRES = 64, 64, 128, 32
LO_W = 384
PR_W = 3 * HW + LO_W
P_EXT = 3 * HW + PR_W
DILATIONS = (1, 4, 16)
N_CLS = 16
N_BACK = 128
BLK = 128
ST_W = HW + 256
ROPE_THETA = 10000.0
D_FF = 4 * D_MODEL
ALPHA = (2.0 * DEPTH) ** 0.25
LN_EPS = 1e-5
GN_EPS = 64e-5
RMS_EPS = 1e-6
CHUNK = 64
VMEM_LIMIT = 56 * 1024 * 1024


def _cparams(*sem):
    return pltpu.CompilerParams(dimension_semantics=sem, vmem_limit_bytes=VMEM_LIMIT)


def _mm(a, b):
    return jnp.dot(a.astype(BF16), b.astype(BF16), preferred_element_type=F32)


def _split2(x):
    hi = x.astype(BF16)
    lo = (x - hi.astype(F32)).astype(BF16)
    return hi, lo


def _sum_heads(x, bd):
    hi, lo = _split2(x)
    d = functools.partial(jnp.dot, preferred_element_type=F32)
    return d(hi, bd) + d(lo, bd)


def _stage_lane_tiles(scr, val):
    for j in range(scr.shape[0]):
        scr[j] = val[:, j * 128:(j + 1) * 128]


def _strided_rows(scr, start, rows, stride):
    return jnp.concatenate([scr[j, pl.ds(start, rows, stride=stride), :] for j in range(scr.shape[0])], axis=1)


def _layer_norm(x, g, b):
    mu = jnp.mean(x, axis=-1, keepdims=True)
    xc = x - mu
    var = jnp.mean(xc * xc, axis=-1, keepdims=True)
    return xc * lax.rsqrt(var + LN_EPS) * g + b


def _rope(p, cos, sin):
    lane = lax.broadcasted_iota(jnp.int32, p.shape, 1) % HEAD_DIM
    rot = jnp.where(lane < HEAD_DIM // 2, pltpu.roll(p, HW - HEAD_DIM // 2, 1),
                    pltpu.roll(p, HEAD_DIM // 2, 1))
    return p * cos + rot * sin


def _proj_kernel(x_ref, w_ref, cos_ref, sin_ref, q_ref, kb_ref, vb_ref, kf_ref, vf_ref, pr_ref, *cm):
    xb = x_ref[...].astype(BF16)
    cos = jnp.concatenate([cos_ref[...]] * 4, axis=1)
    sin = jnp.concatenate([sin_ref[...]] * 4, axis=1)
    d = functools.partial(jnp.dot, preferred_element_type=F32)

    def class_major(val, dst_ref, scr):
        _stage_lane_tiles(scr, val)
        rows = val.shape[0] // N_CLS
        for r in range(N_CLS):
            dst_ref[r] = _strided_rows(scr, r, rows, N_CLS).astype(BF16)

    q = _rope(d(xb, w_ref[:, 0:HW]), cos, sin) * (HEAD_DIM ** -0.5)
    q_ref[...] = q.astype(BF16)
    k = _rope(d(xb, w_ref[:, HW:2 * HW]), cos, sin)
    kf_ref[...] = k
    kb_ref[...] = k.astype(BF16)
    v = d(xb, w_ref[:, 2 * HW:3 * HW])
    vf_ref[...] = v
    vb_ref[...] = v.astype(BF16)
    if cm:
        qc_ref, kc_ref, vc_ref, scr = cm
        class_major(q, qc_ref, scr)
        class_major(k, kc_ref, scr)
        class_major(v, vc_ref, scr)
    pr_ref[...] = d(xb, w_ref[:, 3 * HW:P_EXT])


def _proj(x, w_ext, layer, cos, sin, tm, seq_len=None):
    n = x.shape[0]
    t_tiles = cos.shape[0] // tm
    row = lambda i: (i, 0)
    out_specs = [pl.BlockSpec((tm, HW), row)] * 5 + [pl.BlockSpec((tm, PR_W), row)]
    out_shape = ([jax.ShapeDtypeStruct((n, HW), BF16)] * 3 + [jax.ShapeDtypeStruct((n, HW), F32)] * 2
                 + [jax.ShapeDtypeStruct((n, PR_W), F32)])
    scratch = []
    if seq_len is not None:
        tpb = seq_len // tm
        cm_blk = pl.BlockSpec((None, N_CLS, tm // N_CLS, HW), lambda i: (i // tpb, 0, i % tpb, 0))
        out_specs += [cm_blk] * 3
        out_shape += [jax.ShapeDtypeStruct((n // seq_len, N_CLS, seq_len // N_CLS, HW), BF16)] * 3
        scratch = [pltpu.VMEM((HW // 128, tm, 128), F32)]
    return pl.pallas_call(
        _proj_kernel,
        grid=(n // tm,),
        in_specs=[pl.BlockSpec((tm, D_MODEL), row),
                  pl.BlockSpec((None, D_MODEL, P_EXT), lambda i: (layer, 0, 0)),
                  pl.BlockSpec((tm, 128), lambda i: (i % t_tiles, 0)),
                  pl.BlockSpec((tm, 128), lambda i: (i % t_tiles, 0))],
        out_specs=out_specs,
        out_shape=out_shape,
        scratch_shapes=scratch,
        compiler_params=_cparams("parallel"),
        name="proj",
    )(x, w_ext, cos, sin)


def _proj_rwkv_kernel(x_ref, w_ref, pr_ref):
    pr_ref[...] = jnp.dot(x_ref[...].astype(BF16), w_ref[:, 3 * HW:P_EXT], preferred_element_type=F32)


def _proj_rwkv(x, w_ext, layer):
    n = x.shape[0]
    return pl.pallas_call(
        _proj_rwkv_kernel,
        grid=(1,),
        in_specs=[pl.BlockSpec((n, D_MODEL), lambda i: (0, 0)),
                  pl.BlockSpec((None, D_MODEL, P_EXT), lambda i: (layer, 0, 0))],
        out_specs=pl.BlockSpec((n, PR_W), lambda i: (0, 0)),
        out_shape=jax.ShapeDtypeStruct((n, PR_W), F32),
        compiler_params=_cparams("arbitrary"),
        name="proj_prev",
    )(x, w_ext)


def _att_kernel(*refs, has_prev, is_last, g, scatter_out):
    q_ref, k_ref, v_ref, kp_ref, vp_ref = refs[:5]
    pos = 5
    st_ref = gain_ref = None
    if has_prev:
        st_ref = refs[pos]
        pos += 1
    if is_last:
        gain_ref = refs[pos]
        pos += 1
    out_ref = refs[pos]
    st_scr = refs[pos + 1] if scatter_out else None

    flat = lambda a: a.reshape(BLK, a.shape[-1])
    n = pl.program_id(2)
    q = flat(q_ref[...])
    kc = flat(k_ref[...])
    vc = flat(v_ref[...])
    kp = flat(kp_ref[...])
    vp = flat(vp_ref[...])
    per = BLK // g
    order = lambda i: (i % per) * g + i // per
    qo = order(lax.broadcasted_iota(jnp.int32, (BLK, BLK), 0))
    ko = order(lax.broadcasted_iota(jnp.int32, (BLK, BLK), 1))
    valid_c = ko <= qo
    valid_p = jnp.where(ko >= qo, jnp.where(n > 0, 1, 0), 0) > 0
    nt = (((1,), (1,)), ((), ()))
    sls = [slice(h * HEAD_DIM, (h + 1) * HEAD_DIM) for h in range(N_HEADS)]
    s_c = [jnp.where(valid_c, lax.dot_general(q[:, sl], kc[:, sl], nt, preferred_element_type=F32), -jnp.inf)
           for sl in sls]
    s_p = [jnp.where(valid_p, lax.dot_general(q[:, sl], kp[:, sl], nt, preferred_element_type=F32), -jnp.inf)
           for sl in sls]
    m_h = [jnp.maximum(jnp.max(a, axis=-1, keepdims=True), jnp.max(b, axis=-1, keepdims=True))
           for a, b in zip(s_c, s_p)]
    e_c = [jnp.exp(a - mm) for a, mm in zip(s_c, m_h)]
    e_p = [jnp.exp(a - mm) for a, mm in zip(s_p, m_h)]
    l_h = [jnp.sum(a, axis=-1, keepdims=True) + jnp.sum(b, axis=-1, keepdims=True) for a, b in zip(e_c, e_p)]
    acc = jnp.concatenate([jnp.dot(a.astype(BF16), vc[:, sl], preferred_element_type=F32)
                           + jnp.dot(b.astype(BF16), vp[:, sl], preferred_element_type=F32)
                           for a, b, sl in zip(e_c, e_p, sls)], axis=1)
    lane = lax.broadcasted_iota(jnp.int32, (BLK, 128), 1)
    m = jnp.zeros((BLK, 128), F32)
    l = jnp.ones((BLK, 128), F32)
    for h in range(N_HEADS):
        m = jnp.where(lane == h, m_h[h], m)
        l = jnp.where(lane == h, l_h[h], l)
    head_of_lane = lax.broadcasted_iota(jnp.int32, (128, HW), 1) // HEAD_DIM
    sel = jnp.where(lax.broadcasted_iota(jnp.int32, (128, HW), 0) == head_of_lane, 1.0, 0.0).astype(BF16)

    def expand(x):
        hi = x.astype(BF16)
        r1 = x - hi.astype(F32)
        mid = r1.astype(BF16)
        low = (r1 - mid.astype(F32)).astype(BF16)
        d = functools.partial(jnp.dot, preferred_element_type=F32)
        return d(hi, sel) + d(mid, sel) + d(low, sel)

    if has_prev:
        st = flat(st_ref[...])
        m0 = st[:, HW:HW + 128]
        l0 = st[:, HW + 128:ST_W]
        mn = jnp.maximum(m0, m)
        a0 = jnp.exp(m0 - mn)
        a1 = jnp.exp(m - mn)
        acc = expand(a0) * st[:, 0:HW] + expand(a1) * acc
        l = a0 * l0 + a1 * l
        m = mn
    if is_last:
        att = acc * expand(1.0 / l)
        ms = jnp.mean(att * att, axis=-1, keepdims=True)
        out_ref[...] = (att * lax.rsqrt(ms + RMS_EPS) * gain_ref[...]).astype(BF16).reshape(out_ref.shape)
    elif scatter_out:
        _stage_lane_tiles(st_scr, jnp.concatenate([acc, m, l], axis=1))
        for r in range(N_CLS):
            out_ref[r] = _strided_rows(st_scr, r, BLK // N_CLS, N_CLS)
    else:
        out_ref[...] = jnp.concatenate([acc, m, l], axis=1).reshape(out_ref.shape)


def _att_prompt(q, kb, vb, qc, kc, vc, gain):
    B, T, _ = q.shape
    m16 = T // N_CLS
    call = lambda kern, grid, in_specs, out_spec, out_shape, scratch, name: pl.pallas_call(
        kern, grid=grid, in_specs=in_specs, out_specs=out_spec, out_shape=out_shape,
        scratch_shapes=scratch, compiler_params=_cparams("parallel", "parallel", "arbitrary"), name=name)
    prev = lambda n: jnp.maximum(n - 1, 0)

    blk = pl.BlockSpec((None, BLK, HW), lambda b, r, n: (b, n, 0))
    pblk = pl.BlockSpec((None, BLK, HW), lambda b, r, n: (b, prev(n), 0))
    st1 = call(functools.partial(_att_kernel, has_prev=False, is_last=False, g=1, scatter_out=True),
               (B, 1, T // BLK), [blk] * 3 + [pblk] * 2,
               pl.BlockSpec((None, N_CLS, BLK // N_CLS, ST_W), lambda b, r, n: (b, 0, n, 0)),
               jax.ShapeDtypeStruct((B, N_CLS, m16, ST_W), F32), [pltpu.VMEM((ST_W // 128, BLK, 128), F32)],
               "att_d1")(q, kb, vb, kb, vb)

    g = N_CLS // 4
    v5 = lambda a: a.reshape(B, g, 4, m16, a.shape[-1])
    blk = lambda w: pl.BlockSpec((None, g, None, BLK // g, w), lambda b, r, n: (b, 0, r, n, 0))
    pblk = pl.BlockSpec((None, g, None, BLK // g, HW), lambda b, r, n: (b, 0, r, prev(n), 0))
    st4 = call(functools.partial(_att_kernel, has_prev=True, is_last=False, g=g, scatter_out=False),
               (B, 4, m16 * g // BLK), [blk(HW)] * 3 + [pblk] * 2 + [blk(ST_W)], blk(ST_W),
               jax.ShapeDtypeStruct((B, g, 4, m16, ST_W), F32), [],
               "att_d4")(v5(qc), v5(kc), v5(vc), v5(kc), v5(vc), v5(st1))
    st4 = st4.reshape(B, N_CLS, m16, ST_W)

    blk = lambda w: pl.BlockSpec((None, None, BLK, w), lambda b, r, n: (b, r, n, 0))
    pblk = pl.BlockSpec((None, None, BLK, HW), lambda b, r, n: (b, r, prev(n), 0))
    return call(functools.partial(_att_kernel, has_prev=True, is_last=True, g=1, scatter_out=False),
                (B, N_CLS, m16 // BLK),
                [blk(HW)] * 3 + [pblk] * 2 + [blk(ST_W), pl.BlockSpec((1, HW), lambda b, r, n: (0, 0))],
                blk(HW), jax.ShapeDtypeStruct((B, N_CLS, m16, HW), BF16), [],
                "att_d16")(qc, kc, vc, kc, vc, st4, gain)


def _att_dec_kernel(q_ref, kn_ref, vn_ref, k_ref, v_ref, gain_ref, out_ref):
    L = k_ref.shape[-1]
    dist = L - lax.broadcasted_iota(jnp.int32, (1, L), 1)
    mult = jnp.zeros((1, L), F32)
    for d in DILATIONS:
        mult = mult + jnp.where((dist & (d - 1)) == 0, jnp.where(dist <= d * N_BACK, 1.0, 0.0), 0.0)
    active = mult > 0.0
    q = q_ref[...]
    kn = kn_ref[...]
    vn = vn_ref[...]
    lane = lax.broadcasted_iota(jnp.int32, (HEAD_DIM, N_HEADS), 1)
    o = jnp.zeros((HEAD_DIM, N_HEADS), F32)
    for h in range(N_HEADS):
        qc = q[:, h:h + 1]
        s = jnp.where(active, jnp.sum(k_ref[h] * qc, axis=0, keepdims=True), -jnp.inf)
        s_new = jnp.sum(qc * kn[:, h:h + 1], axis=0, keepdims=True)
        m = jnp.maximum(jnp.max(s, axis=-1, keepdims=True), s_new)
        e = mult * jnp.exp(s - m)
        e_new = jnp.exp(s_new - m) * float(len(DILATIONS))
        l = jnp.sum(e, axis=-1, keepdims=True) + e_new
        acc = jnp.sum(v_ref[h] * e, axis=-1, keepdims=True) + e_new * vn[:, h:h + 1]
        o = jnp.where(lane == h, acc / l, o)
    ms = jnp.sum(jnp.sum(o * o, axis=-1, keepdims=True), axis=0, keepdims=True) * (1.0 / HW)
    out_ref[...] = o * lax.rsqrt(ms + RMS_EPS) * gain_ref[...]


def _att_decode(q, kn, vn, cache_kt, cache_vt, layer, gain):
    bd, L = cache_kt.shape[1], cache_kt.shape[-1]
    row = pl.BlockSpec((None, HEAD_DIM, N_HEADS), lambda b: (b, 0, 0))
    cblk = pl.BlockSpec((None, None, N_HEADS, HEAD_DIM, L), lambda b: (layer, b, 0, 0, 0))
    return pl.pallas_call(
        _att_dec_kernel,
        grid=(bd,),
        in_specs=[row, row, row, cblk, cblk, pl.BlockSpec((HEAD_DIM, N_HEADS), lambda b: (0, 0))],
        out_specs=row,
        out_shape=jax.ShapeDtypeStruct((bd, HEAD_DIM, N_HEADS), F32),
        compiler_params=_cparams("parallel"),
        name="att_dec",
    )(q, kn, vn, cache_kt, cache_vt, gain)


def _softplus(x):
    return jnp.maximum(x, 0.0) + jnp.log(1.0 + jnp.exp(-jnp.abs(x)))


def _rwkv_pre(z, vfirst, prm, has_vres):
    (dbase, up1, ibase, gup, vbase, vup, ksk, ksa, bd) = prm
    r = z[:, 0:HW]
    k = z[:, HW:2 * HW]
    v = z[:, 2 * HW:3 * HW]
    lo1 = z[:, 3 * HW:3 * HW + 128]
    lane = lax.broadcasted_iota(jnp.int32, lo1.shape, 1)
    lo1 = jnp.where(lane < LORA_DECAY, jnp.tanh(lo1), lo1)
    up = _mm(lo1, up1)
    w = -_softplus(-(dbase + up[:, 0:HW])) - 0.5
    logw = -jnp.exp(w)
    iclr = jax.nn.sigmoid(ibase + up[:, HW:2 * HW])
    g = _mm(jax.nn.sigmoid(z[:, 3 * HW + 128:3 * HW + 256]), gup)
    if has_vres:
        mix = jax.nn.sigmoid(vbase + _mm(z[:, 3 * HW + 256:3 * HW + 384], vup))
        v = v + (vfirst - v) * mix
    kk = k * ksk
    nrm = jnp.sqrt(_sum_heads(kk * kk, bd))
    kk = kk / jnp.maximum(nrm, 1e-12)
    k = k * (1.0 + (iclr - 1.0) * ksa)
    return r, k, v, logw, -kk, kk * iclr, g


def _rwkv_post(y, r, k, v, g, brk, gng, gnb, bd):
    mu = _sum_heads(y, bd) * (1.0 / HEAD_DIM)
    yc = y - mu
    var = _sum_heads(yc * yc, bd) * (1.0 / HEAD_DIM)
    yn = yc * lax.rsqrt(var + GN_EPS) * gng + gnb
    bonus = _sum_heads(r * k * brk, bd) * v
    return (yn + bonus) * g


def _bmm(a, b):
    return jnp.einsum('bij,bjk->bik', a.astype(BF16), b.astype(BF16), preferred_element_type=F32)


def _bmm_hi(a, b):
    ah, al = _split2(a)
    bh, bl = _split2(b)
    d = functools.partial(jnp.einsum, 'bij,bjk->bik', preferred_element_type=F32)
    return d(ah, bh) + d(ah, bl) + d(al, bh)


def _rwkv_chunk(at, rt, btT, ktT, vv, pcb, h0, masks):
    strict, incl, eye, levels = masks
    c = CHUNK
    qr = jnp.concatenate([at, rt], axis=1)
    gb = _bmm(qr, btT)
    gk = _bmm(qr, ktT)
    a_ab = jnp.where(strict, gb[:, 0:c], 0.0)
    a_rb = jnp.where(incl, gb[:, c:2 * c], 0.0)
    a_ak = jnp.where(strict, gk[:, 0:c], 0.0)
    a_rk = jnp.where(incl, gk[:, c:2 * c], 0.0)
    t = eye + jnp.where(levels[0], a_ab, 0.0)
    for lv in levels[1:]:
        t = t + _bmm(_bmm(t, jnp.where(lv, a_ab, 0.0)), t)
    xv = _bmm(jnp.concatenate([a_ak, a_rk, ktT], axis=1), vv)
    w = _bmm(t, at)
    z = _bmm(t, xv[:, 0:c])
    lhs = jnp.concatenate([a_rb, btT], axis=1)
    lw = _bmm(lhs, w)
    lz = _bmm(lhs, z)
    rp = rt + lw[:, 0:c]
    y0 = lz[:, 0:c] + xv[:, c:2 * c]
    m = pcb * (eye + lw[:, c:2 * c])
    n = pcb * (lz[:, c:2 * c] + xv[:, 2 * c:3 * c])
    y = _bmm(rp, h0) + y0
    h1 = _bmm_hi(m, h0) + n
    return y, h1


def _chunk_masks():
    ri = lax.broadcasted_iota(jnp.int32, (CHUNK, CHUNK), 0)
    ci = lax.broadcasted_iota(jnp.int32, (CHUNK, CHUNK), 1)
    strict = ri > ci
    incl = ri >= ci
    eye = jnp.where(ri == ci, 1.0, 0.0).astype(F32)
    levels = []
    size = 2
    while size <= CHUNK:
        same = (ri // size) == (ci // size)
        if size == 2:
            levels.append(same)
        else:
            inner = (ri // (size // 2)) == (ci // (size // 2))
            levels.append(jnp.where(same, jnp.where(inner, 0, 1), 0) > 0)
        size *= 2
    return strict, incl, eye, levels


def _rwkv_kernel(*refs, has_vres, tb):
    pr_ref = refs[0]
    pos = 1
    vf_ref = None
    if has_vres:
        vf_ref = refs[pos]
        pos += 1
    (mu_ref, dbase, up1, ibase, gup, vbase, vup, ksk, ksa, brk, gng, gnb, bd_ref, tri_ref) = refs[pos:pos + 14]
    pos += 14
    rw_ref = refs[pos]
    pos += 1
    vfo_ref = None
    if not has_vres:
        vfo_ref = refs[pos]
        pos += 1
    so_ref = refs[pos]
    pos += 1
    h_scr, carry, at4, rt4, bt4, kt4, v4, pc4, y4, y_s = refs[pos:pos + 10]
    nc = tb // CHUNK

    i = pl.program_id(1)

    @pl.when(i == 0)
    def _():
        h_scr[...] = jnp.zeros_like(h_scr)
        carry[...] = jnp.zeros_like(carry)

    p = pr_ref[...]
    row = lax.broadcasted_iota(jnp.int32, p.shape, 0)
    prev = jnp.where(row == 0, carry[0:1, :], pltpu.roll(p, 1, 0))
    carry[0:1, :] = p[tb - 1:tb, :]
    z = p + (prev - p) * mu_ref[...]
    bd = bd_ref[...]
    prm = (dbase[...], up1[...], ibase[...], gup[...], vbase[...], vup[...], ksk[...], ksa[...], bd)
    r, k, v, logw, a, b, g = _rwkv_pre(z, vf_ref[...] if has_vres else None, prm, has_vres)
    if not has_vres:
        vfo_ref[...] = v
    hi = logw.astype(BF16)
    r1 = logw - hi.astype(F32)
    mid = r1.astype(BF16)
    low = (r1 - mid.astype(F32)).astype(BF16)
    tri = tri_ref[...]
    d = functools.partial(jnp.dot, preferred_element_type=F32)
    cum = d(tri, hi) + d(tri, mid) + d(tri, low)
    ecum = jnp.exp(cum)
    einv = jnp.exp(-cum)
    a_t = a * jnp.exp(cum - logw)
    r_t = r * ecum
    btT = (b * einv).T
    ktT = (k * einv).T
    pcT = ecum.T
    for c in range(nc):
        rs = slice(c * CHUNK, (c + 1) * CHUNK)
        for h in range(N_HEADS):
            ls = slice(h * HEAD_DIM, (h + 1) * HEAD_DIM)
            at4[c, h] = a_t[rs, ls]
            rt4[c, h] = r_t[rs, ls]
            v4[c, h] = v[rs, ls]
            bt4[c, h] = btT[ls, rs]
            kt4[c, h] = ktT[ls, rs]
            pc4[c, h] = jnp.broadcast_to(pcT[ls, (c + 1) * CHUNK - 1:(c + 1) * CHUNK], (HEAD_DIM, CHUNK))
    masks = _chunk_masks()

    def chunk(c, _):
        y, h1 = _rwkv_chunk(at4[c], rt4[c], bt4[c], kt4[c], v4[c], pc4[c], h_scr[...], masks)
        y4[c] = y
        h_scr[...] = h1
        return 0

    lax.fori_loop(0, nc, chunk, 0)
    for c in range(nc):
        for h in range(N_HEADS):
            y_s[c * CHUNK:(c + 1) * CHUNK, h * HEAD_DIM:(h + 1) * HEAD_DIM] = y4[c, h]
    rw = _rwkv_post(y_s[...], r, k, v, g, brk[...], gng[...], gnb[...], bd)
    rw_ref[...] = rw.astype(BF16)

    @pl.when(i == pl.num_programs(1) - 1)
    def _():
        hst = h_scr[...]
        hi = hst.astype(BF16)
        r1 = hst - hi.astype(F32)
        mid = r1.astype(BF16)
        low = (r1 - mid.astype(F32)).astype(BF16)
        eye = jnp.broadcast_to(masks[2].astype(BF16)[None], (N_HEADS, HEAD_DIM, HEAD_DIM))
        d = functools.partial(jnp.einsum, 'bik,bjk->bij', preferred_element_type=F32)
        so_ref[...] = d(eye, hi) + d(eye, mid) + d(eye, low)


def _rwkv_prompt(pr, vfirst, wts, tb):
    B, T, _ = pr.shape
    has_vres = vfirst is not None
    blk = lambda w: pl.BlockSpec((None, tb, w), lambda b, i: (b, i, 0))
    const = lambda a: pl.BlockSpec(a.shape, lambda b, i: (0,) * a.ndim)
    args, specs = [pr], [blk(PR_W)]
    if has_vres:
        args.append(vfirst)
        specs.append(blk(HW))
    idx = jnp.arange(tb)
    tri = ((idx[:, None] // CHUNK == idx[None, :] // CHUNK) & (idx[:, None] >= idx[None, :])).astype(BF16)
    for a in wts + (tri,):
        args.append(a)
        specs.append(const(a))
    out_shape = [jax.ShapeDtypeStruct((B, T, HW), BF16)]
    out_specs = [blk(HW)]
    if not has_vres:
        out_shape.append(jax.ShapeDtypeStruct((B, T, HW), F32))
        out_specs.append(blk(HW))
    out_shape.append(jax.ShapeDtypeStruct((B, N_HEADS, HEAD_DIM, HEAD_DIM), F32))
    out_specs.append(pl.BlockSpec((None, N_HEADS, HEAD_DIM, HEAD_DIM), lambda b, i: (b, 0, 0, 0)))
    per_head = lambda: pltpu.VMEM((tb // CHUNK, N_HEADS, HEAD_DIM, HEAD_DIM), F32)
    scratch = [pltpu.VMEM((N_HEADS, HEAD_DIM, HEAD_DIM), F32), pltpu.VMEM((8, PR_W), F32)]
    scratch += [per_head() for _ in range(7)] + [pltpu.VMEM((tb, HW), F32)]
    outs = pl.pallas_call(
        functools.partial(_rwkv_kernel, has_vres=has_vres, tb=tb),
        grid=(B, T // tb),
        in_specs=specs,
        out_specs=out_specs,
        out_shape=out_shape,
        scratch_shapes=scratch,
        compiler_params=_cparams("parallel", "arbitrary"),
        name="rwkv",
    )(*args)
    if has_vres:
        return outs[0], vfirst, outs[1]
    return outs[0], outs[1], outs[2]


def _rwkv_dec_kernel(*refs, has_vres):
    pr_ref, pv_ref = refs[0:2]
    pos = 2
    vf_ref = None
    if has_vres:
        vf_ref = refs[pos]
        pos += 1
    s_ref = refs[pos]
    pos += 1
    (mu_ref, dbase, up1, ibase, gup, vbase, vup, ksk, ksa, brk, gng, gnb, bd_ref) = refs[pos:pos + 13]
    pos += 13
    rw_ref = refs[pos]
    pos += 1
    vfo_ref = None
    if not has_vres:
        vfo_ref = refs[pos]
        pos += 1
    so_ref, y_s = refs[pos:pos + 2]

    p = pr_ref[...]
    z = p + (pv_ref[...] - p) * mu_ref[...]
    bd = bd_ref[...]
    prm = (dbase[...], up1[...], ibase[...], gup[...], vbase[...], vup[...], ksk[...], ksa[...], bd)
    r, k, v, logw, a, b, g = _rwkv_pre(z, vf_ref[...] if has_vres else None, prm, has_vres)
    if not has_vres:
        vfo_ref[...] = v
    w = jnp.exp(logw)
    ri = lax.broadcasted_iota(jnp.int32, (HEAD_DIM, HEAD_DIM), 0)
    ci = lax.broadcasted_iota(jnp.int32, (HEAD_DIM, HEAD_DIM), 1)
    eye = jnp.where(ri == ci, 1.0, 0.0).astype(F32)
    for h in range(N_HEADS):
        sl = slice(h * HEAD_DIM, (h + 1) * HEAD_DIM)
        s = s_ref[h]
        sa = jnp.sum(s * a[:, sl], axis=-1, keepdims=True)
        vcol = jnp.sum(eye * v[:, sl], axis=-1, keepdims=True)
        s = s * w[:, sl] + sa * b[:, sl] + vcol * k[:, sl]
        so_ref[h] = s
        ycol = jnp.sum(s * r[:, sl], axis=-1, keepdims=True)
        y_s[:, sl] = jnp.sum(eye * ycol, axis=0, keepdims=True)
    rw = _rwkv_post(y_s[...], r, k, v, g, brk[...], gng[...], gnb[...], bd)
    rw_ref[...] = rw.astype(BF16)


def _rwkv_decode(pr, prev, vfirst, state, layer, wts):
    bd_ = pr.shape[0]
    has_vres = vfirst is not None
    row = lambda w: pl.BlockSpec((None, 1, w), lambda b: (b, 0, 0))
    const = lambda a: pl.BlockSpec(a.shape, lambda b: (0,) * a.ndim)
    sblk = (None, N_HEADS, HEAD_DIM, HEAD_DIM)
    args, specs = [pr, prev], [row(PR_W), row(PR_W)]
    if has_vres:
        args.append(vfirst)
        specs.append(row(HW))
    args.append(state)
    specs.append(pl.BlockSpec((None,) + sblk, lambda b: (layer, b, 0, 0, 0)))
    for a in wts:
        args.append(a)
        specs.append(const(a))
    out_shape = [jax.ShapeDtypeStruct((bd_, 1, HW), BF16)]
    out_specs = [row(HW)]
    if not has_vres:
        out_shape.append(jax.ShapeDtypeStruct((bd_, 1, HW), F32))
        out_specs.append(row(HW))
    out_shape.append(jax.ShapeDtypeStruct((bd_, N_HEADS, HEAD_DIM, HEAD_DIM), F32))
    out_specs.append(pl.BlockSpec(sblk, lambda b: (b, 0, 0, 0)))
    outs = pl.pallas_call(
        functools.partial(_rwkv_dec_kernel, has_vres=has_vres),
        grid=(bd_,),
        in_specs=specs,
        out_specs=out_specs,
        out_shape=out_shape,
        scratch_shapes=[pltpu.VMEM((1, HW), F32)],
        compiler_params=_cparams("parallel"),
        name="rwkv_dec",
    )(*args)
    if has_vres:
        return outs[0], vfirst, outs[1]
    return outs[0], outs[1], outs[2]


def _mix_kernel(att_ref, rw_ref, x_ref, wo_ref, g_ref, b_ref, out_ref, *scr):
    d = functools.partial(jnp.dot, preferred_element_type=F32)
    if scr:
        nat, = scr
        rows = att_ref.shape[1]
        for r in range(N_CLS):
            blk = att_ref[r].astype(F32)
            for j in range(nat.shape[0]):
                nat[j, pl.ds(r, rows, stride=N_CLS), :] = blk[:, j * 128:(j + 1) * 128]
        att = jnp.concatenate([nat[j] for j in range(nat.shape[0])], axis=1).astype(BF16)
    else:
        att = att_ref[...]
    mix = d(att, wo_ref[0:HW, :]) + d(rw_ref[...], wo_ref[HW:2 * HW, :])
    out_ref[...] = _layer_norm(ALPHA * x_ref[...] + mix, g_ref[...], b_ref[...])


def _mix(att, rw, x, w_out, layer, g, b, tm):
    n = x.shape[0]
    row = lambda w: pl.BlockSpec((tm, w), lambda i: (i, 0))
    vec = pl.BlockSpec((None, 1, D_MODEL), lambda i: (layer, 0, 0))
    scratch = []
    att_spec = row(HW)
    if att.ndim == 4:
        tpb = att.shape[2] * N_CLS // tm
        att_spec = pl.BlockSpec((None, N_CLS, tm // N_CLS, HW), lambda i: (i // tpb, 0, i % tpb, 0))
        scratch = [pltpu.VMEM((HW // 128, tm, 128), F32)]
    return pl.pallas_call(
        _mix_kernel,
        grid=(n // tm,),
        in_specs=[att_spec, row(HW), row(D_MODEL),
                  pl.BlockSpec((None, D_MODEL, D_MODEL), lambda i: (layer, 0, 0)), vec, vec],
        out_specs=row(D_MODEL),
        out_shape=jax.ShapeDtypeStruct((n, D_MODEL), F32),
        scratch_shapes=scratch,
        compiler_params=_cparams("parallel"),
        name="mix",
    )(att, rw, x, w_out, g, b)


FF_CHUNK = 1024


def _ffn_kernel(x_ref, wu_ref, wd_ref, g_ref, b_ref, out_ref):
    x = x_ref[...]
    xb = x.astype(BF16)
    acc = ALPHA * x
    for j in range(D_FF // FF_CHUNK):
        cs = slice(j * FF_CHUNK, (j + 1) * FF_CHUNK)
        h = jnp.maximum(jnp.dot(xb, wu_ref[:, cs], preferred_element_type=F32), 0.0)
        acc = acc + jnp.dot((h * h).astype(BF16), wd_ref[cs, :], preferred_element_type=F32)
    out_ref[...] = _layer_norm(acc, g_ref[...], b_ref[...])


def _ffn(x, w_up, w_down, layer, g, b, tm):
    n = x.shape[0]
    row = pl.BlockSpec((tm, D_MODEL), lambda i: (i, 0))
    vec = pl.BlockSpec((None, 1, D_MODEL), lambda i: (layer, 0, 0))
    return pl.pallas_call(
        _ffn_kernel,
        grid=(n // tm,),
        in_specs=[row, pl.BlockSpec((None, D_MODEL, D_FF), lambda i: (layer, 0, 0)),
                  pl.BlockSpec((None, D_FF, D_MODEL), lambda i: (layer, 0, 0)), vec, vec],
        out_specs=row,
        out_shape=jax.ShapeDtypeStruct((n, D_MODEL), F32),
        compiler_params=_cparams("parallel"),
        name="ffn",
    )(x, w_up, w_down, g, b)


def _rope_tables(pos):
    half = HEAD_DIM // 2
    inv = ROPE_THETA ** (-jnp.arange(half, dtype=F32) * (2.0 / HEAD_DIM))
    ang = pos.astype(F32)[:, None] * inv[None, :]
    cos, sin = jnp.cos(ang), jnp.sin(ang)
    cos2 = jnp.concatenate([cos, cos, cos, cos], axis=1)
    sin2 = jnp.concatenate([-sin, sin, -sin, sin], axis=1)
    return cos2, sin2


def kernel(x_prompt, x_sample, state_shift, state_wkv, cache_k, cache_v, w_in, w_vres_in, shift_mu, vres_mu, decay_base, decay_up, iclr_base, iclr_up, gate_up, vres_base, vres_up, key_scale_k, key_scale_a, bonus_rk, gn_g, gn_b, att_gain, w_out, ln1_g, ln1_b, w_ff_up, w_ff_down, ln2_g, ln2_b):
    B, T, _ = x_prompt.shape
    bd_, ds_, _ = x_sample.shape
    L = cache_k.shape[2]
    past_len = 8192
    assert ds_ == 1 and T % (DILATIONS[-1] * BLK) == 0 and L == DILATIONS[-1] * N_BACK
    keep = min(DILATIONS[-1] * N_BACK, T)
    tm_p, tb = 512, 256

    z32 = jnp.zeros((1, D_MODEL, LORA_VRES), F32)
    w_ext = jnp.concatenate([w_in, jnp.concatenate([z32, w_vres_in], 0),
                             jnp.zeros((DEPTH, D_MODEL, LO_W - 288), F32)], axis=-1).astype(BF16)
    mu_ext = jnp.concatenate([shift_mu, jnp.concatenate([jnp.zeros((1, LORA_VRES), F32), vres_mu], 0),
                              jnp.zeros((DEPTH, LO_W - 288), F32)], axis=-1)
    zdu = jnp.zeros((DEPTH, LORA_DECAY, HW), F32)
    up1 = jnp.concatenate([jnp.concatenate([decay_up, zdu], -1), jnp.concatenate([zdu, iclr_up], -1)],
                          axis=1).astype(BF16)
    gup = gate_up.astype(BF16)
    vup = jnp.concatenate([jnp.concatenate([jnp.zeros((1, LORA_VRES, HW), F32), vres_up], 0),
                           jnp.zeros((DEPTH, 128 - LORA_VRES, HW), F32)], axis=1).astype(BF16)
    vbase = jnp.concatenate([jnp.zeros((1, HW), F32), vres_base], 0)
    hid = jnp.arange(HW) // HEAD_DIM
    bd = (hid[:, None] == hid[None, :]).astype(BF16)
    w_out_b = w_out.astype(BF16)
    w_up_b = w_ff_up.astype(BF16)
    w_down_b = w_ff_down.astype(BF16)
    r3 = lambda a: a.reshape(DEPTH, 1, a.shape[-1])
    ln1g, ln1b, ln2g, ln2b = r3(ln1_g), r3(ln1_b), r3(ln2_g), r3(ln2_b)

    cos_p, sin_p = _rope_tables(jnp.arange(T))
    cos_s, sin_s = _rope_tables(jnp.full((bd_,), past_len))
    cache_kt = jnp.transpose(cache_k, (0, 1, 3, 4, 2))
    cache_vt = jnp.transpose(cache_v, (0, 1, 3, 4, 2))

    hp = x_prompt.reshape(B * T, D_MODEL)
    hs = x_sample.reshape(bd_, D_MODEL)
    vf_p = vf_s = None
    shift_p, shift_s, wkv_p, wkv_s, kp_l, vp_l, ks_l, vs_l = [], [], [], [], [], [], [], []
    for l in range(DEPTH):
        row1 = lambda a: a[l].reshape(1, -1)
        wts = (row1(mu_ext), row1(decay_base), up1[l], row1(iclr_base), gup[l], row1(vbase), vup[l],
               row1(key_scale_k), row1(key_scale_a), row1(bonus_rk), row1(gn_g), row1(gn_b), bd)
        gain = row1(att_gain)
        shift_p.append(hp.reshape(B, T, D_MODEL)[:, -1])
        shift_s.append(hs)

        q, kb, vb, kf, vf, pr, qc, kc, vc = _proj(hp, w_ext, l, cos_p, sin_p, tm_p, seq_len=T)
        r3d = lambda a: a.reshape(B, T, a.shape[-1])
        att = _att_prompt(r3d(q), r3d(kb), r3d(vb), qc, kc, vc, gain)
        rw, vf_p, s_p = _rwkv_prompt(r3d(pr), vf_p, wts, tb)
        x1 = _mix(att, rw.reshape(B * T, HW), hp, w_out_b, l, ln1g, ln1b, tm_p)
        hp = _ffn(x1, w_up_b, w_down_b, l, ln2g, ln2b, 256)
        wkv_p.append(s_p)
        kp_l.append(r3d(kf)[:, T - keep:].reshape(B, keep, N_HEADS, HEAD_DIM))
        vp_l.append(r3d(vf)[:, T - keep:].reshape(B, keep, N_HEADS, HEAD_DIM))

        qs, kbs, vbs, kfs, vfs, prs = _proj(hs, w_ext, l, cos_s, sin_s, bd_)
        prev = _proj_rwkv(state_shift[l], w_ext, l)
        e1 = lambda a: a.reshape(bd_, 1, a.shape[-1])
        hd = lambda a: a.astype(F32).reshape(-1, N_HEADS, HEAD_DIM).swapaxes(-1, -2)
        att_s = _att_decode(hd(qs), hd(kfs), hd(vfs), cache_kt, cache_vt, l, hd(gain)[0])
        att_s = att_s.swapaxes(-1, -2)
        rw_s, vf_s, s_s = _rwkv_decode(e1(prs), e1(prev), vf_s, state_wkv, l, wts)
        x1s = _mix(att_s.reshape(bd_, HW).astype(BF16), rw_s.reshape(bd_, HW), hs, w_out_b, l, ln1g, ln1b, bd_)
        hs = _ffn(x1s, w_up_b, w_down_b, l, ln2g, ln2b, bd_)
        wkv_s.append(s_s)
        ks_l.append(kfs.reshape(bd_, 1, N_HEADS, HEAD_DIM))
        vs_l.append(vfs.reshape(bd_, 1, N_HEADS, HEAD_DIM))

    return (hp.reshape(B, T, D_MODEL), hs.reshape(bd_, 1, D_MODEL), jnp.stack(shift_p), jnp.stack(shift_s),
            jnp.stack(wkv_p), jnp.stack(wkv_s), jnp.stack(kp_l), jnp.stack(vp_l), jnp.stack(ks_l),
            jnp.stack(vs_l))
```

```python
import functools

import jax
import jax.numpy as jnp
from jax import lax
from jax.experimental import pallas as pl
from jax.experimental.pallas import tpu as pltpu

F32 = jnp.float32
BF16 = jnp.bfloat16

D_MODEL = 1024
DEPTH = 4
HEAD_DIM = 64
N_HEADS = 8
HW = N_HEADS * HEAD_DIM
LORA_DECAY, LORA_ICLR, LORA_GATE, LORA_VRES = 64, 64, 128, 32
LO_W = 384
PR_W = 3 * HW + LO_W
P_EXT = 3 * HW + PR_W
DILATIONS = (1, 4, 16)
N_CLS = 16
N_BACK = 128
BLK = 128
ST_W = HW + 256
ROPE_THETA = 10000.0
D_FF = 4 * D_MODEL
ALPHA = (2.0 * DEPTH) ** 0.25
LN_EPS = 1e-5
GN_EPS = 64e-5
RMS_EPS = 1e-6
CHUNK = 64
RWKV_GROUP = 4
ATT_TPS = 4
VMEM_LIMIT = 56 * 1024 * 1024


def _cparams(*sem):
    return pltpu.CompilerParams(dimension_semantics=sem, vmem_limit_bytes=VMEM_LIMIT)


def _mm(a, b):
    return jnp.dot(a.astype(BF16), b.astype(BF16), preferred_element_type=F32)


def _split2(x):
    hi = x.astype(BF16)
    lo = (x - hi.astype(F32)).astype(BF16)
    return hi, lo


def _sum_heads(x, bd):
    hi, lo = _split2(x)
    d = functools.partial(jnp.dot, preferred_element_type=F32)
    return d(hi, bd) + d(lo, bd)


def _stage_lane_tiles(scr, val):
    for j in range(scr.shape[0]):
        scr[j] = val[:, j * 128:(j + 1) * 128]


def _strided_rows(scr, start, rows, stride):
    return jnp.concatenate([scr[j, pl.ds(start, rows, stride=stride), :] for j in range(scr.shape[0])], axis=1)


def _layer_norm(x, g, b):
    mu = jnp.mean(x, axis=-1, keepdims=True)
    xc = x - mu
    var = jnp.mean(xc * xc, axis=-1, keepdims=True)
    return xc * lax.rsqrt(var + LN_EPS) * g + b


def _rope(p, cos, sin):
    lane = lax.broadcasted_iota(jnp.int32, p.shape, 1) % HEAD_DIM
    rot = jnp.where(lane < HEAD_DIM // 2, pltpu.roll(p, HW - HEAD_DIM // 2, 1),
                    pltpu.roll(p, HEAD_DIM // 2, 1))
    return p * cos + rot * sin


def _proj_kernel(x_ref, w_ref, cos_ref, sin_ref, q_ref, kb_ref, vb_ref, kf_ref, vf_ref, pr_ref, *cm):
    xb = x_ref[...].astype(BF16)
    cos = jnp.concatenate([cos_ref[...]] * 4, axis=1)
    sin = jnp.concatenate([sin_ref[...]] * 4, axis=1)
    d = functools.partial(jnp.dot, preferred_element_type=F32)

    def class_major(val, dst_ref, scr):
        _stage_lane_tiles(scr, val)
        rows = val.shape[0] // N_CLS
        for r in range(N_CLS):
            dst_ref[r] = _strided_rows(scr, r, rows, N_CLS).astype(BF16)

    q = _rope(d(xb, w_ref[:, 0:HW]), cos, sin) * (HEAD_DIM ** -0.5)
    q_ref[...] = q.astype(BF16)
    k = _rope(d(xb, w_ref[:, HW:2 * HW]), cos, sin)
    kf_ref[...] = k
    kb_ref[...] = k.astype(BF16)
    v = d(xb, w_ref[:, 2 * HW:3 * HW])
    vf_ref[...] = v
    vb_ref[...] = v.astype(BF16)
    if cm:
        qc_ref, kc_ref, vc_ref, scr = cm
        class_major(q, qc_ref, scr)
        class_major(k, kc_ref, scr)
        class_major(v, vc_ref, scr)
    pr_ref[...] = d(xb, w_ref[:, 3 * HW:P_EXT])


def _proj(x, w_ext, layer, cos, sin, tm, seq_len=None):
    n = x.shape[0]
    t_tiles = cos.shape[0] // tm
    row = lambda i: (i, 0)
    out_specs = [pl.BlockSpec((tm, HW), row)] * 5 + [pl.BlockSpec((tm, PR_W), row)]
    out_shape = ([jax.ShapeDtypeStruct((n, HW), BF16)] * 3 + [jax.ShapeDtypeStruct((n, HW), F32)] * 2
                 + [jax.ShapeDtypeStruct((n, PR_W), F32)])
    scratch = []
    if seq_len is not None:
        tpb = seq_len // tm
        cm_blk = pl.BlockSpec((None, N_CLS, tm // N_CLS, HW), lambda i: (i // tpb, 0, i % tpb, 0))
        out_specs += [cm_blk] * 3
        out_shape += [jax.ShapeDtypeStruct((n // seq_len, N_CLS, seq_len // N_CLS, HW), BF16)] * 3
        scratch = [pltpu.VMEM((HW // 128, tm, 128), F32)]
    return pl.pallas_call(
        _proj_kernel,
        grid=(n // tm,),
        in_specs=[pl.BlockSpec((tm, D_MODEL), row),
                  pl.BlockSpec((None, D_MODEL, P_EXT), lambda i: (layer, 0, 0)),
                  pl.BlockSpec((tm, 128), lambda i: (i % t_tiles, 0)),
                  pl.BlockSpec((tm, 128), lambda i: (i % t_tiles, 0))],
        out_specs=out_specs,
        out_shape=out_shape,
        scratch_shapes=scratch,
        compiler_params=_cparams("parallel"),
        name="proj",
    )(x, w_ext, cos, sin)


def _proj_rwkv_kernel(x_ref, w_ref, pr_ref):
    pr_ref[...] = jnp.dot(x_ref[...].astype(BF16), w_ref[:, 3 * HW:P_EXT], preferred_element_type=F32)


def _proj_rwkv(x, w_ext, layer):
    n = x.shape[0]
    return pl.pallas_call(
        _proj_rwkv_kernel,
        grid=(1,),
        in_specs=[pl.BlockSpec((n, D_MODEL), lambda i: (0, 0)),
                  pl.BlockSpec((None, D_MODEL, P_EXT), lambda i: (layer, 0, 0))],
        out_specs=pl.BlockSpec((n, PR_W), lambda i: (0, 0)),
        out_shape=jax.ShapeDtypeStruct((n, PR_W), F32),
        compiler_params=_cparams("arbitrary"),
        name="proj_prev",
    )(x, w_ext)


def _att_kernel(*refs, has_prev, is_last, g, scatter_out, tps):
    q_ref, k_ref, v_ref, kp_ref, vp_ref = refs[:5]
    pos = 5
    st_ref = gain_ref = None
    if has_prev:
        st_ref = refs[pos]
        pos += 1
    if is_last:
        gain_ref = refs[pos]
        pos += 1
    out_ref = refs[pos]
    st_scr = refs[pos + 1] if scatter_out else None

    per = BLK // g
    n = pl.program_id(2)

    def tile(ref, j):
        if g == 1:
            return ref[j * BLK:(j + 1) * BLK, :]
        return ref[:, j * per:(j + 1) * per, :].reshape(BLK, ref.shape[-1])

    qs = [tile(q_ref, j) for j in range(tps)]
    ks = [tile(k_ref, j) for j in range(tps)]
    vs = [tile(v_ref, j) for j in range(tps)]
    kps = [kp_ref[...].reshape(BLK, HW)] + ks[:-1]
    vps = [vp_ref[...].reshape(BLK, HW)] + vs[:-1]
    order = lambda i: (i % per) * g + i // per
    qo = order(lax.broadcasted_iota(jnp.int32, (BLK, BLK), 0))
    ko = order(lax.broadcasted_iota(jnp.int32, (BLK, BLK), 1))
    valid_c = ko <= qo
    valid_p = [jnp.where(ko >= qo, jnp.where(n > 0, 1, 0), 0) > 0] + [ko >= qo] * (tps - 1)
    nt = (((1,), (1,)), ((), ()))
    pairs = [(j, slice(h * HEAD_DIM, (h + 1) * HEAD_DIM)) for j in range(tps) for h in range(N_HEADS)]
    s_c = [jnp.where(valid_c, lax.dot_general(qs[j][:, sl], ks[j][:, sl], nt, preferred_element_type=F32),
                     -jnp.inf) for j, sl in pairs]
    s_p = [jnp.where(valid_p[j], lax.dot_general(qs[j][:, sl], kps[j][:, sl], nt, preferred_element_type=F32),
                     -jnp.inf) for j, sl in pairs]
    m_h = [jnp.maximum(jnp.max(a, axis=-1, keepdims=True), jnp.max(b, axis=-1, keepdims=True))
           for a, b in zip(s_c, s_p)]
    e_c = [jnp.exp(a - mm) for a, mm in zip(s_c, m_h)]
    e_p = [jnp.exp(a - mm) for a, mm in zip(s_p, m_h)]
    l_h = [jnp.sum(a, axis=-1, keepdims=True) + jnp.sum(b, axis=-1, keepdims=True) for a, b in zip(e_c, e_p)]
    pv = [jnp.dot(a.astype(BF16), vs[j][:, sl], preferred_element_type=F32)
          + jnp.dot(b.astype(BF16), vps[j][:, sl], preferred_element_type=F32)
          for a, b, (j, sl) in zip(e_c, e_p, pairs)]
    head_of_lane = lax.broadcasted_iota(jnp.int32, (128, HW), 1) // HEAD_DIM
    sel = jnp.where(lax.broadcasted_iota(jnp.int32, (128, HW), 0) == head_of_lane, 1.0, 0.0).astype(BF16)

    def expand(x):
        hi = x.astype(BF16)
        r1 = x - hi.astype(F32)
        mid = r1.astype(BF16)
        low = (r1 - mid.astype(F32)).astype(BF16)
        d = functools.partial(jnp.dot, preferred_element_type=F32)
        return d(hi, sel) + d(mid, sel) + d(low, sel)

    lane = lax.broadcasted_iota(jnp.int32, (BLK, 128), 1)
    for j in range(tps):
        acc = jnp.concatenate(pv[j * N_HEADS:(j + 1) * N_HEADS], axis=1)
        m = jnp.zeros((BLK, 128), F32)
        l = jnp.ones((BLK, 128), F32)
        for h in range(N_HEADS):
            m = jnp.where(lane == h, m_h[j * N_HEADS + h], m)
            l = jnp.where(lane == h, l_h[j * N_HEADS + h], l)
        if has_prev:
            st = tile(st_ref, j)
            m0 = st[:, HW:HW + 128]
            l0 = st[:, HW + 128:ST_W]
            mn = jnp.maximum(m0, m)
            a0 = jnp.exp(m0 - mn)
            a1 = jnp.exp(m - mn)
            acc = expand(a0) * st[:, 0:HW] + expand(a1) * acc
            l = a0 * l0 + a1 * l
            m = mn
        if is_last:
            att = acc * expand(1.0 / l)
            ms = jnp.mean(att * att, axis=-1, keepdims=True)
            out_ref[j * BLK:(j + 1) * BLK, :] = (att * lax.rsqrt(ms + RMS_EPS) * gain_ref[...]).astype(BF16)
        elif scatter_out:
            stage = st_scr.at[j]
            nrow = BLK // N_CLS
            _stage_lane_tiles(stage, jnp.concatenate([acc, m, l], axis=1))
            for r in range(N_CLS):
                out_ref[r, j * nrow:(j + 1) * nrow, :] = _strided_rows(stage, r, nrow, N_CLS)
        else:
            out_ref[:, j * per:(j + 1) * per, :] = jnp.concatenate([acc, m, l], axis=1).reshape(g, per, ST_W)


def _att_prompt(q, kb, vb, qc, kc, vc, gain):
    B, T, _ = q.shape
    m16 = T // N_CLS
    call = lambda kern, grid, in_specs, out_spec, out_shape, scratch, name: pl.pallas_call(
        kern, grid=grid, in_specs=in_specs, out_specs=out_spec, out_shape=out_shape,
        scratch_shapes=scratch, compiler_params=_cparams("parallel", "parallel", "arbitrary"), name=name)
    tps = ATT_TPS
    prev = lambda n: jnp.maximum(tps * n - 1, 0)
    kern = lambda **kw: functools.partial(_att_kernel, tps=tps, **kw)

    blk = pl.BlockSpec((None, BLK * tps, HW), lambda b, r, n: (b, n, 0))
    pblk = pl.BlockSpec((None, BLK, HW), lambda b, r, n: (b, prev(n), 0))
    st1 = call(kern(has_prev=False, is_last=False, g=1, scatter_out=True),
               (B, 1, T // (BLK * tps)), [blk] * 3 + [pblk] * 2,
               pl.BlockSpec((None, N_CLS, BLK * tps // N_CLS, ST_W), lambda b, r, n: (b, 0, n, 0)),
               jax.ShapeDtypeStruct((B, N_CLS, m16, ST_W), F32),
               [pltpu.VMEM((tps, ST_W // 128, BLK, 128), F32)],
               "att_d1")(q, kb, vb, kb, vb)

    g = N_CLS // 4
    per = BLK // g
    v5 = lambda a: a.reshape(B, g, 4, m16, a.shape[-1])
    blk = lambda w: pl.BlockSpec((None, g, None, per * tps, w), lambda b, r, n: (b, 0, r, n, 0))
    pblk = pl.BlockSpec((None, g, None, per, HW), lambda b, r, n: (b, 0, r, prev(n), 0))
    st4 = call(kern(has_prev=True, is_last=False, g=g, scatter_out=False),
               (B, 4, m16 // (per * tps)), [blk(HW)] * 3 + [pblk] * 2 + [blk(ST_W)], blk(ST_W),
               jax.ShapeDtypeStruct((B, g, 4, m16, ST_W), F32), [],
               "att_d4")(v5(qc), v5(kc), v5(vc), v5(kc), v5(vc), v5(st1))
    st4 = st4.reshape(B, N_CLS, m16, ST_W)

    blk = lambda w: pl.BlockSpec((None, None, BLK * tps, w), lambda b, r, n: (b, r, n, 0))
    pblk = pl.BlockSpec((None, None, BLK, HW), lambda b, r, n: (b, r, prev(n), 0))
    return call(kern(has_prev=True, is_last=True, g=1, scatter_out=False),
                (B, N_CLS, m16 // (BLK * tps)),
                [blk(HW)] * 3 + [pblk] * 2 + [blk(ST_W), pl.BlockSpec((1, HW), lambda b, r, n: (0, 0))],
                blk(HW), jax.ShapeDtypeStruct((B, N_CLS, m16, HW), BF16), [],
                "att_d16")(qc, kc, vc, kc, vc, st4, gain)


def _att_dec_kernel(q_ref, kn_ref, vn_ref, k_ref, v_ref, gain_ref, out_ref):
    L = k_ref.shape[-1]
    dist = L - lax.broadcasted_iota(jnp.int32, (1, L), 1)
    mult = jnp.zeros((1, L), F32)
    for d in DILATIONS:
        mult = mult + jnp.where((dist & (d - 1)) == 0, jnp.where(dist <= d * N_BACK, 1.0, 0.0), 0.0)
    active = mult > 0.0
    q = q_ref[...]
    kn = kn_ref[...]
    vn = vn_ref[...]
    lane = lax.broadcasted_iota(jnp.int32, (HEAD_DIM, N_HEADS), 1)
    o = jnp.zeros((HEAD_DIM, N_HEADS), F32)
    for h in range(N_HEADS):
        qc = q[:, h:h + 1]
        s = jnp.where(active, jnp.sum(k_ref[h] * qc, axis=0, keepdims=True), -jnp.inf)
        s_new = jnp.sum(qc * kn[:, h:h + 1], axis=0, keepdims=True)
        m = jnp.maximum(jnp.max(s, axis=-1, keepdims=True), s_new)
        e = mult * jnp.exp(s - m)
        e_new = jnp.exp(s_new - m) * float(len(DILATIONS))
        l = jnp.sum(e, axis=-1, keepdims=True) + e_new
        acc = jnp.sum(v_ref[h] * e, axis=-1, keepdims=True) + e_new * vn[:, h:h + 1]
        o = jnp.where(lane == h, acc / l, o)
    ms = jnp.sum(jnp.sum(o * o, axis=-1, keepdims=True), axis=0, keepdims=True) * (1.0 / HW)
    out_ref[...] = o * lax.rsqrt(ms + RMS_EPS) * gain_ref[...]


def _att_decode(q, kn, vn, cache_kt, cache_vt, layer, gain):
    bd, L = cache_kt.shape[1], cache_kt.shape[-1]
    row = pl.BlockSpec((None, HEAD_DIM, N_HEADS), lambda b: (b, 0, 0))
    cblk = pl.BlockSpec((None, None, N_HEADS, HEAD_DIM, L), lambda b: (layer, b, 0, 0, 0))
    return pl.pallas_call(
        _att_dec_kernel,
        grid=(bd,),
        in_specs=[row, row, row, cblk, cblk, pl.BlockSpec((HEAD_DIM, N_HEADS), lambda b: (0, 0))],
        out_specs=row,
        out_shape=jax.ShapeDtypeStruct((bd, HEAD_DIM, N_HEADS), F32),
        compiler_params=_cparams("parallel"),
        name="att_dec",
    )(q, kn, vn, cache_kt, cache_vt, gain)


def _softplus(x):
    return jnp.maximum(x, 0.0) + jnp.log(1.0 + jnp.exp(-jnp.abs(x)))


def _rwkv_pre(z, vfirst, prm, has_vres):
    (dbase, up1, ibase, gup, vbase, vup, ksk, ksa, bd) = prm
    r = z[:, 0:HW]
    k = z[:, HW:2 * HW]
    v = z[:, 2 * HW:3 * HW]
    lo1 = z[:, 3 * HW:3 * HW + 128]
    lane = lax.broadcasted_iota(jnp.int32, lo1.shape, 1)
    lo1 = jnp.where(lane < LORA_DECAY, jnp.tanh(lo1), lo1)
    up = _mm(lo1, up1)
    w = -_softplus(-(dbase + up[:, 0:HW])) - 0.5
    logw = -jnp.exp(w)
    iclr = jax.nn.sigmoid(ibase + up[:, HW:2 * HW])
    g = _mm(jax.nn.sigmoid(z[:, 3 * HW + 128:3 * HW + 256]), gup)
    if has_vres:
        mix = jax.nn.sigmoid(vbase + _mm(z[:, 3 * HW + 256:3 * HW + 384], vup))
        v = v + (vfirst - v) * mix
    kk = k * ksk
    nrm = jnp.sqrt(_sum_heads(kk * kk, bd))
    kk = kk / jnp.maximum(nrm, 1e-12)
    k = k * (1.0 + (iclr - 1.0) * ksa)
    return r, k, v, logw, -kk, kk * iclr, g


def _rwkv_post(y, r, k, v, g, brk, gng, gnb, bd):
    mu = _sum_heads(y, bd) * (1.0 / HEAD_DIM)
    yc = y - mu
    var = _sum_heads(yc * yc, bd) * (1.0 / HEAD_DIM)
    yn = yc * lax.rsqrt(var + GN_EPS) * gng + gnb
    bonus = _sum_heads(r * k * brk, bd) * v
    return (yn + bonus) * g


def _bmm(a, b):
    return jnp.einsum('bij,bjk->bik', a.astype(BF16), b.astype(BF16), preferred_element_type=F32)


def _bmm_hi(a, b):
    ah, al = _split2(a)
    bh, bl = _split2(b)
    d = functools.partial(jnp.einsum, 'bij,bjk->bik', preferred_element_type=F32)
    return d(ah, bh) + d(ah, bl) + d(al, bh)


def _rwkv_chunk(at, rt, btT, ktT, vv, pcb, masks):
    strict, incl, eye, levels = masks
    c = CHUNK
    qr = jnp.concatenate([at, rt], axis=1)
    gb = _bmm(qr, btT)
    gk = _bmm(qr, ktT)
    a_ab = jnp.where(strict, gb[:, 0:c], 0.0)
    a_rb = jnp.where(incl, gb[:, c:2 * c], 0.0)
    a_ak = jnp.where(strict, gk[:, 0:c], 0.0)
    a_rk = jnp.where(incl, gk[:, c:2 * c], 0.0)
    t = eye + jnp.where(levels[0], a_ab, 0.0)
    for lv in levels[1:]:
        t = t + _bmm(_bmm(t, jnp.where(lv, a_ab, 0.0)), t)
    xv = _bmm(jnp.concatenate([a_ak, a_rk, ktT], axis=1), vv)
    w = _bmm(t, at)
    z = _bmm(t, xv[:, 0:c])
    lhs = jnp.concatenate([a_rb, btT], axis=1)
    lw = _bmm(lhs, w)
    lz = _bmm(lhs, z)
    rp = rt + lw[:, 0:c]
    y0 = lz[:, 0:c] + xv[:, c:2 * c]
    m = pcb * (eye + lw[:, c:2 * c])
    n = pcb * (lz[:, c:2 * c] + xv[:, 2 * c:3 * c])
    return rp, y0, m, n


def _chunk_masks():
    ri = lax.broadcasted_iota(jnp.int32, (CHUNK, CHUNK), 0)
    ci = lax.broadcasted_iota(jnp.int32, (CHUNK, CHUNK), 1)
    strict = ri > ci
    incl = ri >= ci
    eye = jnp.where(ri == ci, 1.0, 0.0).astype(F32)
    levels = []
    size = 2
    while size <= CHUNK:
        same = (ri // size) == (ci // size)
        if size == 2:
            levels.append(same)
        else:
            inner = (ri // (size // 2)) == (ci // (size // 2))
            levels.append(jnp.where(same, jnp.where(inner, 0, 1), 0) > 0)
        size *= 2
    return strict, incl, eye, levels


def _rwkv_kernel(*refs, has_vres, tb):
    pr_ref = refs[0]
    pos = 1
    vf_ref = None
    if has_vres:
        vf_ref = refs[pos]
        pos += 1
    (mu_ref, dbase, up1, ibase, gup, vbase, vup, ksk, ksa, brk, gng, gnb, bd_ref, tri_ref) = refs[pos:pos + 14]
    pos += 14
    rw_ref = refs[pos]
    pos += 1
    vfo_ref = None
    if not has_vres:
        vfo_ref = refs[pos]
        pos += 1
    so_ref = refs[pos]
    pos += 1
    h_scr, carry, at4, rt4, bt4, kt4, v4, pc4, y4, y_s = refs[pos:pos + 10]
    nc = tb // CHUNK

    i = pl.program_id(1)

    @pl.when(i == 0)
    def _():
        h_scr[...] = jnp.zeros_like(h_scr)
        carry[...] = jnp.zeros_like(carry)

    p = pr_ref[...]
    row = lax.broadcasted_iota(jnp.int32, p.shape, 0)
    prev = jnp.where(row == 0, carry[0:1, :], pltpu.roll(p, 1, 0))
    carry[0:1, :] = p[tb - 1:tb, :]
    z = p + (prev - p) * mu_ref[...]
    bd = bd_ref[...]
    prm = (dbase[...], up1[...], ibase[...], gup[...], vbase[...], vup[...], ksk[...], ksa[...], bd)
    r, k, v, logw, a, b, g = _rwkv_pre(z, vf_ref[...] if has_vres else None, prm, has_vres)
    if not has_vres:
        vfo_ref[...] = v
    hi = logw.astype(BF16)
    r1 = logw - hi.astype(F32)
    mid = r1.astype(BF16)
    low = (r1 - mid.astype(F32)).astype(BF16)
    tri = tri_ref[...]
    d = functools.partial(jnp.dot, preferred_element_type=F32)
    cum = d(tri, hi) + d(tri, mid) + d(tri, low)
    ecum = jnp.exp(cum)
    einv = jnp.exp(-cum)
    a_t = a * jnp.exp(cum - logw)
    r_t = r * ecum
    btT = (b * einv).T
    ktT = (k * einv).T
    pcT = ecum.T
    for c in range(nc):
        rs = slice(c * CHUNK, (c + 1) * CHUNK)
        for h in range(N_HEADS):
            ls = slice(h * HEAD_DIM, (h + 1) * HEAD_DIM)
            at4[c, h] = a_t[rs, ls]
            rt4[c, h] = r_t[rs, ls]
            v4[c, h] = v[rs, ls]
            bt4[c, h] = btT[ls, rs]
            kt4[c, h] = ktT[ls, rs]
            pc4[c, h] = jnp.broadcast_to(pcT[ls, (c + 1) * CHUNK - 1:(c + 1) * CHUNK], (HEAD_DIM, CHUNK))
    masks = _chunk_masks()

    def chunks(i, _):
        c0 = i * RWKV_GROUP
        grp = lambda ref: ref[pl.ds(c0, RWKV_GROUP)].reshape(RWKV_GROUP * N_HEADS, HEAD_DIM, HEAD_DIM)
        rp, y0, m, n = _rwkv_chunk(grp(at4), grp(rt4), grp(bt4), grp(kt4), grp(v4), grp(pc4), masks)
        h = h_scr[...]
        for j in range(RWKV_GROUP):
            hs = slice(j * N_HEADS, (j + 1) * N_HEADS)
            y4[c0 + j] = _bmm(rp[hs], h) + y0[hs]
            h = _bmm_hi(m[hs], h) + n[hs]
        h_scr[...] = h
        return 0

    lax.fori_loop(0, nc // RWKV_GROUP, chunks, 0)
    for c in range(nc):
        for h in range(N_HEADS):
            y_s[c * CHUNK:(c + 1) * CHUNK, h * HEAD_DIM:(h + 1) * HEAD_DIM] = y4[c, h]
    rw = _rwkv_post(y_s[...], r, k, v, g, brk[...], gng[...], gnb[...], bd)
    rw_ref[...] = rw.astype(BF16)

    @pl.when(i == pl.num_programs(1) - 1)
    def _():
        hst = h_scr[...]
        hi = hst.astype(BF16)
        r1 = hst - hi.astype(F32)
        mid = r1.astype(BF16)
        low = (r1 - mid.astype(F32)).astype(BF16)
        eye = jnp.broadcast_to(masks[2].astype(BF16)[None], (N_HEADS, HEAD_DIM, HEAD_DIM))
        d = functools.partial(jnp.einsum, 'bik,bjk->bij', preferred_element_type=F32)
        so_ref[...] = d(eye, hi) + d(eye, mid) + d(eye, low)


def _rwkv_prompt(pr, vfirst, wts, tb):
    B, T, _ = pr.shape
    has_vres = vfirst is not None
    blk = lambda w: pl.BlockSpec((None, tb, w), lambda b, i: (b, i, 0))
    const = lambda a: pl.BlockSpec(a.shape, lambda b, i: (0,) * a.ndim)
    args, specs = [pr], [blk(PR_W)]
    if has_vres:
        args.append(vfirst)
        specs.append(blk(HW))
    idx = jnp.arange(tb)
    tri = ((idx[:, None] // CHUNK == idx[None, :] // CHUNK) & (idx[:, None] >= idx[None, :])).astype(BF16)
    for a in wts + (tri,):
        args.append(a)
        specs.append(const(a))
    out_shape = [jax.ShapeDtypeStruct((B, T, HW), BF16)]
    out_specs = [blk(HW)]
    if not has_vres:
        out_shape.append(jax.ShapeDtypeStruct((B, T, HW), F32))
        out_specs.append(blk(HW))
    out_shape.append(jax.ShapeDtypeStruct((B, N_HEADS, HEAD_DIM, HEAD_DIM), F32))
    out_specs.append(pl.BlockSpec((None, N_HEADS, HEAD_DIM, HEAD_DIM), lambda b, i: (b, 0, 0, 0)))
    per_head = lambda: pltpu.VMEM((tb // CHUNK, N_HEADS, HEAD_DIM, HEAD_DIM), F32)
    scratch = [pltpu.VMEM((N_HEADS, HEAD_DIM, HEAD_DIM), F32), pltpu.VMEM((8, PR_W), F32)]
    scratch += [per_head() for _ in range(7)] + [pltpu.VMEM((tb, HW), F32)]
    outs = pl.pallas_call(
        functools.partial(_rwkv_kernel, has_vres=has_vres, tb=tb),
        grid=(B, T // tb),
        in_specs=specs,
        out_specs=out_specs,
        out_shape=out_shape,
        scratch_shapes=scratch,
        compiler_params=_cparams("parallel", "arbitrary"),
        name="rwkv",
    )(*args)
    if has_vres:
        return outs[0], vfirst, outs[1]
    return outs[0], outs[1], outs[2]


def _rwkv_dec_kernel(*refs, has_vres):
    pr_ref, pv_ref = refs[0:2]
    pos = 2
    vf_ref = None
    if has_vres:
        vf_ref = refs[pos]
        pos += 1
    s_ref = refs[pos]
    pos += 1
    (mu_ref, dbase, up1, ibase, gup, vbase, vup, ksk, ksa, brk, gng, gnb, bd_ref) = refs[pos:pos + 13]
    pos += 13
    rw_ref = refs[pos]
    pos += 1
    vfo_ref = None
    if not has_vres:
        vfo_ref = refs[pos]
        pos += 1
    so_ref, y_s = refs[pos:pos + 2]

    p = pr_ref[...]
    z = p + (pv_ref[...] - p) * mu_ref[...]
    bd = bd_ref[...]
    prm = (dbase[...], up1[...], ibase[...], gup[...], vbase[...], vup[...], ksk[...], ksa[...], bd)
    r, k, v, logw, a, b, g = _rwkv_pre(z, vf_ref[...] if has_vres else None, prm, has_vres)
    if not has_vres:
        vfo_ref[...] = v
    w = jnp.exp(logw)
    ri = lax.broadcasted_iota(jnp.int32, (HEAD_DIM, HEAD_DIM), 0)
    ci = lax.broadcasted_iota(jnp.int32, (HEAD_DIM, HEAD_DIM), 1)
    eye = jnp.where(ri == ci, 1.0, 0.0).astype(F32)
    for h in range(N_HEADS):
        sl = slice(h * HEAD_DIM, (h + 1) * HEAD_DIM)
        s = s_ref[h]
        sa = jnp.sum(s * a[:, sl], axis=-1, keepdims=True)
        vcol = jnp.sum(eye * v[:, sl], axis=-1, keepdims=True)
        s = s * w[:, sl] + sa * b[:, sl] + vcol * k[:, sl]
        so_ref[h] = s
        ycol = jnp.sum(s * r[:, sl], axis=-1, keepdims=True)
        y_s[:, sl] = jnp.sum(eye * ycol, axis=0, keepdims=True)
    rw = _rwkv_post(y_s[...], r, k, v, g, brk[...], gng[...], gnb[...], bd)
    rw_ref[...] = rw.astype(BF16)


def _rwkv_decode(pr, prev, vfirst, state, layer, wts):
    bd_ = pr.shape[0]
    has_vres = vfirst is not None
    row = lambda w: pl.BlockSpec((None, 1, w), lambda b: (b, 0, 0))
    const = lambda a: pl.BlockSpec(a.shape, lambda b: (0,) * a.ndim)
    sblk = (None, N_HEADS, HEAD_DIM, HEAD_DIM)
    args, specs = [pr, prev], [row(PR_W), row(PR_W)]
    if has_vres:
        args.append(vfirst)
        specs.append(row(HW))
    args.append(state)
    specs.append(pl.BlockSpec((None,) + sblk, lambda b: (layer, b, 0, 0, 0)))
    for a in wts:
        args.append(a)
        specs.append(const(a))
    out_shape = [jax.ShapeDtypeStruct((bd_, 1, HW), BF16)]
    out_specs = [row(HW)]
    if not has_vres:
        out_shape.append(jax.ShapeDtypeStruct((bd_, 1, HW), F32))
        out_specs.append(row(HW))
    out_shape.append(jax.ShapeDtypeStruct((bd_, N_HEADS, HEAD_DIM, HEAD_DIM), F32))
    out_specs.append(pl.BlockSpec(sblk, lambda b: (b, 0, 0, 0)))
    outs = pl.pallas_call(
        functools.partial(_rwkv_dec_kernel, has_vres=has_vres),
        grid=(bd_,),
        in_specs=specs,
        out_specs=out_specs,
        out_shape=out_shape,
        scratch_shapes=[pltpu.VMEM((1, HW), F32)],
        compiler_params=_cparams("parallel"),
        name="rwkv_dec",
    )(*args)
    if has_vres:
        return outs[0], vfirst, outs[1]
    return outs[0], outs[1], outs[2]


def _mix_kernel(att_ref, rw_ref, x_ref, wo_ref, g_ref, b_ref, out_ref, *scr):
    d = functools.partial(jnp.dot, preferred_element_type=F32)
    if scr:
        nat, = scr
        rows = att_ref.shape[1]
        for r in range(N_CLS):
            blk = att_ref[r].astype(F32)
            for j in range(nat.shape[0]):
                nat[j, pl.ds(r, rows, stride=N_CLS), :] = blk[:, j * 128:(j + 1) * 128]
        att = jnp.concatenate([nat[j] for j in range(nat.shape[0])], axis=1).astype(BF16)
    else:
        att = att_ref[...]
    mix = d(att, wo_ref[0:HW, :]) + d(rw_ref[...], wo_ref[HW:2 * HW, :])
    out_ref[...] = _layer_norm(ALPHA * x_ref[...] + mix, g_ref[...], b_ref[...])


def _mix(att, rw, x, w_out, layer, g, b, tm):
    n = x.shape[0]
    row = lambda w: pl.BlockSpec((tm, w), lambda i: (i, 0))
    vec = pl.BlockSpec((None, 1, D_MODEL), lambda i: (layer, 0, 0))
    scratch = []
    att_spec = row(HW)
    if att.ndim == 4:
        tpb = att.shape[2] * N_CLS // tm
        att_spec = pl.BlockSpec((None, N_CLS, tm // N_CLS, HW), lambda i: (i // tpb, 0, i % tpb, 0))
        scratch = [pltpu.VMEM((HW // 128, tm, 128), F32)]
    return pl.pallas_call(
        _mix_kernel,
        grid=(n // tm,),
        in_specs=[att_spec, row(HW), row(D_MODEL),
                  pl.BlockSpec((None, D_MODEL, D_MODEL), lambda i: (layer, 0, 0)), vec, vec],
        out_specs=row(D_MODEL),
        out_shape=jax.ShapeDtypeStruct((n, D_MODEL), F32),
        scratch_shapes=scratch,
        compiler_params=_cparams("parallel"),
        name="mix",
    )(att, rw, x, w_out, g, b)


FF_CHUNK = 1024


def _ffn_kernel(x_ref, wu_ref, wd_ref, g_ref, b_ref, out_ref):
    x = x_ref[...]
    xb = x.astype(BF16)
    acc = ALPHA * x
    for j in range(D_FF // FF_CHUNK):
        cs = slice(j * FF_CHUNK, (j + 1) * FF_CHUNK)
        h = jnp.maximum(jnp.dot(xb, wu_ref[:, cs], preferred_element_type=F32), 0.0)
        acc = acc + jnp.dot((h * h).astype(BF16), wd_ref[cs, :], preferred_element_type=F32)
    out_ref[...] = _layer_norm(acc, g_ref[...], b_ref[...])


def _ffn(x, w_up, w_down, layer, g, b, tm):
    n = x.shape[0]
    row = pl.BlockSpec((tm, D_MODEL), lambda i: (i, 0))
    vec = pl.BlockSpec((None, 1, D_MODEL), lambda i: (layer, 0, 0))
    return pl.pallas_call(
        _ffn_kernel,
        grid=(n // tm,),
        in_specs=[row, pl.BlockSpec((None, D_MODEL, D_FF), lambda i: (layer, 0, 0)),
                  pl.BlockSpec((None, D_FF, D_MODEL), lambda i: (layer, 0, 0)), vec, vec],
        out_specs=row,
        out_shape=jax.ShapeDtypeStruct((n, D_MODEL), F32),
        compiler_params=_cparams("parallel"),
        name="ffn",
    )(x, w_up, w_down, g, b)


def _rope_tables(pos):
    half = HEAD_DIM // 2
    inv = ROPE_THETA ** (-jnp.arange(half, dtype=F32) * (2.0 / HEAD_DIM))
    ang = pos.astype(F32)[:, None] * inv[None, :]
    cos, sin = jnp.cos(ang), jnp.sin(ang)
    cos2 = jnp.concatenate([cos, cos, cos, cos], axis=1)
    sin2 = jnp.concatenate([-sin, sin, -sin, sin], axis=1)
    return cos2, sin2


def kernel(x_prompt, x_sample, state_shift, state_wkv, cache_k, cache_v, w_in, w_vres_in, shift_mu, vres_mu, decay_base, decay_up, iclr_base, iclr_up, gate_up, vres_base, vres_up, key_scale_k, key_scale_a, bonus_rk, gn_g, gn_b, att_gain, w_out, ln1_g, ln1_b, w_ff_up, w_ff_down, ln2_g, ln2_b):
    B, T, _ = x_prompt.shape
    bd_, ds_, _ = x_sample.shape
    L = cache_k.shape[2]
    past_len = 8192
    assert ds_ == 1 and T % (N_CLS * BLK * ATT_TPS) == 0
    keep = min(DILATIONS[-1] * N_BACK, T)
    tm_p, tb = 512, 256

    z32 = jnp.zeros((1, D_MODEL, LORA_VRES), F32)
    w_ext = jnp.concatenate([w_in, jnp.concatenate([z32, w_vres_in], 0),
                             jnp.zeros((DEPTH, D_MODEL, LO_W - 288), F32)], axis=-1).astype(BF16)
    mu_ext = jnp.concatenate([shift_mu, jnp.concatenate([jnp.zeros((1, LORA_VRES), F32), vres_mu], 0),
                              jnp.zeros((DEPTH, LO_W - 288), F32)], axis=-1)
    zdu = jnp.zeros((DEPTH, LORA_DECAY, HW), F32)
    up1 = jnp.concatenate([jnp.concatenate([decay_up, zdu], -1), jnp.concatenate([zdu, iclr_up], -1)],
                          axis=1).astype(BF16)
    gup = gate_up.astype(BF16)
    vup = jnp.concatenate([jnp.concatenate([jnp.zeros((1, LORA_VRES, HW), F32), vres_up], 0),
                           jnp.zeros((DEPTH, 128 - LORA_VRES, HW), F32)], axis=1).astype(BF16)
    vbase = jnp.concatenate([jnp.zeros((1, HW), F32), vres_base], 0)
    hid = jnp.arange(HW) // HEAD_DIM
    bd = (hid[:, None] == hid[None, :]).astype(BF16)
    w_out_b = w_out.astype(BF16)
    w_up_b = w_ff_up.astype(BF16)
    w_down_b = w_ff_down.astype(BF16)
    r3 = lambda a: a.reshape(DEPTH, 1, a.shape[-1])
    ln1g, ln1b, ln2g, ln2b = r3(ln1_g), r3(ln1_b), r3(ln2_g), r3(ln2_b)

    cos_p, sin_p = _rope_tables(jnp.arange(T))
    cos_s, sin_s = _rope_tables(jnp.full((bd_,), past_len))
    cache_kt = jnp.transpose(cache_k, (0, 1, 3, 4, 2))
    cache_vt = jnp.transpose(cache_v, (0, 1, 3, 4, 2))

    hp = x_prompt.reshape(B * T, D_MODEL)
    hs = x_sample.reshape(bd_, D_MODEL)
    vf_p = vf_s = None
    shift_p, shift_s, wkv_p, wkv_s, kp_l, vp_l, ks_l, vs_l = [], [], [], [], [], [], [], []
    for l in range(DEPTH):
        row1 = lambda a: a[l].reshape(1, -1)
        wts = (row1(mu_ext), row1(decay_base), up1[l], row1(iclr_base), gup[l], row1(vbase), vup[l],
               row1(key_scale_k), row1(key_scale_a), row1(bonus_rk), row1(gn_g), row1(gn_b), bd)
        gain = row1(att_gain)
        shift_p.append(hp.reshape(B, T, D_MODEL)[:, -1])
        shift_s.append(hs)

        q, kb, vb, kf, vf, pr, qc, kc, vc = _proj(hp, w_ext, l, cos_p, sin_p, tm_p, seq_len=T)
        r3d = lambda a: a.reshape(B, T, a.shape[-1])
        att = _att_prompt(r3d(q), r3d(kb), r3d(vb), qc, kc, vc, gain)
        rw, vf_p, s_p = _rwkv_prompt(r3d(pr), vf_p, wts, tb)
        x1 = _mix(att, rw.reshape(B * T, HW), hp, w_out_b, l, ln1g, ln1b, tm_p)
        hp = _ffn(x1, w_up_b, w_down_b, l, ln2g, ln2b, 256)
        wkv_p.append(s_p)
        kp_l.append(r3d(kf)[:, T - keep:].reshape(B, keep, N_HEADS, HEAD_DIM))
        vp_l.append(r3d(vf)[:, T - keep:].reshape(B, keep, N_HEADS, HEAD_DIM))

        qs, kbs, vbs, kfs, vfs, prs = _proj(hs, w_ext, l, cos_s, sin_s, bd_)
        prev = _proj_rwkv(state_shift[l], w_ext, l)
        e1 = lambda a: a.reshape(bd_, 1, a.shape[-1])
        hd = lambda a: a.astype(F32).reshape(-1, N_HEADS, HEAD_DIM).swapaxes(-1, -2)
        att_s = _att_decode(hd(qs), hd(kfs), hd(vfs), cache_kt, cache_vt, l, hd(gain)[0])
        att_s = att_s.swapaxes(-1, -2)
        rw_s, vf_s, s_s = _rwkv_decode(e1(prs), e1(prev), vf_s, state_wkv, l, wts)
        x1s = _mix(att_s.reshape(bd_, HW).astype(BF16), rw_s.reshape(bd_, HW), hs, w_out_b, l, ln1g, ln1b, bd_)
        hs = _ffn(x1s, w_up_b, w_down_b, l, ln2g, ln2b, bd_)
        wkv_s.append(s_s)
        ks_l.append(kfs.reshape(bd_, 1, N_HEADS, HEAD_DIM))
        vs_l.append(vfs.reshape(bd_, 1, N_HEADS, HEAD_DIM))

    return (hp.reshape(B, T, D_MODEL), hs.reshape(bd_, 1, D_MODEL), jnp.stack(shift_p), jnp.stack(shift_s),
            jnp.stack(wkv_p), jnp.stack(wkv_s), jnp.stack(kp_l), jnp.stack(vp_l), jnp.stack(ks_l),
            jnp.stack(vs_l))
```

```python
import functools

import jax
import jax.numpy as jnp
from jax import lax
from jax.experimental import pallas as pl
from jax.experimental.pallas import tpu as pltpu

F32 = jnp.float32
BF16 = jnp.bfloat16

D_MODEL = 1024
DEPTH = 4
HEAD_DIM = 64
N_HEADS = 8
HW = N_HEADS * HEAD_DIM
LORA_DECAY, LORA_ICLR, LORA_GATE, LORA_VRES = 64, 64, 128, 32
LO_W = 384
PR_W = 3 * HW + LO_W
P_EXT = 3 * HW + PR_W
DILATIONS = (1, 4, 16)
N_CLS = 16
N_BACK = 128
BLK = 128
ST_W = HW + 256
ROPE_THETA = 10000.0
D_FF = 4 * D_MODEL
ALPHA = (2.0 * DEPTH) ** 0.25
LN_EPS = 1e-5
GN_EPS = 64e-5
RMS_EPS = 1e-6
CHUNK = 64
RWKV_GROUP = 4
ATT_TPS = 4
VMEM_LIMIT = 56 * 1024 * 1024


def _cparams(*sem):
    return pltpu.CompilerParams(dimension_semantics=sem, vmem_limit_bytes=VMEM_LIMIT)


def _mm(a, b):
    return jnp.dot(a.astype(BF16), b.astype(BF16), preferred_element_type=F32)


def _split2(x):
    hi = x.astype(BF16)
    lo = (x - hi.astype(F32)).astype(BF16)
    return hi, lo


def _sum_heads(x, bd):
    hi, lo = _split2(x)
    d = functools.partial(jnp.dot, preferred_element_type=F32)
    return d(hi, bd) + d(lo, bd)


def _stage_lane_tiles(scr, val):
    for j in range(scr.shape[0]):
        scr[j] = val[:, j * 128:(j + 1) * 128]


def _strided_rows(scr, start, rows, stride):
    return jnp.concatenate([scr[j, pl.ds(start, rows, stride=stride), :] for j in range(scr.shape[0])], axis=1)


def _layer_norm(x, g, b):
    mu = jnp.mean(x, axis=-1, keepdims=True)
    xc = x - mu
    var = jnp.mean(xc * xc, axis=-1, keepdims=True)
    return xc * lax.rsqrt(var + LN_EPS) * g + b


def _rope(p, cos, sin):
    lane = lax.broadcasted_iota(jnp.int32, p.shape, 1) % HEAD_DIM
    rot = jnp.where(lane < HEAD_DIM // 2, pltpu.roll(p, HW - HEAD_DIM // 2, 1),
                    pltpu.roll(p, HEAD_DIM // 2, 1))
    return p * cos + rot * sin


def _proj_kernel(x_ref, w_ref, cos_ref, sin_ref, q_ref, kb_ref, vb_ref, kf_ref, vf_ref, pr_ref, *cm):
    xb = x_ref[...].astype(BF16)
    cos = jnp.concatenate([cos_ref[...]] * 4, axis=1)
    sin = jnp.concatenate([sin_ref[...]] * 4, axis=1)
    d = functools.partial(jnp.dot, preferred_element_type=F32)

    def class_major(val, dst_ref, scr):
        _stage_lane_tiles(scr, val)
        rows = val.shape[0] // N_CLS
        for r in range(N_CLS):
            dst_ref[r] = _strided_rows(scr, r, rows, N_CLS).astype(BF16)

    q = _rope(d(xb, w_ref[:, 0:HW]), cos, sin) * (HEAD_DIM ** -0.5)
    q_ref[...] = q.astype(BF16)
    k = _rope(d(xb, w_ref[:, HW:2 * HW]), cos, sin)
    kf_ref[...] = k
    kb_ref[...] = k.astype(BF16)
    v = d(xb, w_ref[:, 2 * HW:3 * HW])
    vf_ref[...] = v
    vb_ref[...] = v.astype(BF16)
    if cm:
        qc_ref, kc_ref, vc_ref, scr = cm
        class_major(q, qc_ref, scr)
        class_major(k, kc_ref, scr)
        class_major(v, vc_ref, scr)
    pr_ref[...] = d(xb, w_ref[:, 3 * HW:P_EXT])


def _proj(x, w_ext, layer, cos, sin, tm, seq_len=None, keep=None):
    n = x.shape[0]
    t_tiles = cos.shape[0] // tm
    row = lambda i: (i, 0)
    kv_spec, kv_rows = pl.BlockSpec((tm, HW), row), n
    if seq_len is not None:
        tpb, ktiles = seq_len // tm, keep // tm
        kv_spec = pl.BlockSpec(
            (tm, HW), lambda i: ((i // tpb) * ktiles + jnp.maximum(i % tpb - (tpb - ktiles), 0), 0))
        kv_rows = n // seq_len * keep
    out_specs = [pl.BlockSpec((tm, HW), row)] * 3 + [kv_spec] * 2 + [pl.BlockSpec((tm, PR_W), row)]
    out_shape = ([jax.ShapeDtypeStruct((n, HW), BF16)] * 3 + [jax.ShapeDtypeStruct((kv_rows, HW), F32)] * 2
                 + [jax.ShapeDtypeStruct((n, PR_W), F32)])
    scratch = []
    if seq_len is not None:
        tpb = seq_len // tm
        cm_blk = pl.BlockSpec((None, N_CLS, tm // N_CLS, HW), lambda i: (i // tpb, 0, i % tpb, 0))
        out_specs += [cm_blk] * 3
        out_shape += [jax.ShapeDtypeStruct((n // seq_len, N_CLS, seq_len // N_CLS, HW), BF16)] * 3
        scratch = [pltpu.VMEM((HW // 128, tm, 128), F32)]
    return pl.pallas_call(
        _proj_kernel,
        grid=(n // tm,),
        in_specs=[pl.BlockSpec((tm, D_MODEL), row),
                  pl.BlockSpec((None, D_MODEL, P_EXT), lambda i: (layer, 0, 0)),
                  pl.BlockSpec((tm, 128), lambda i: (i % t_tiles, 0)),
                  pl.BlockSpec((tm, 128), lambda i: (i % t_tiles, 0))],
        out_specs=out_specs,
        out_shape=out_shape,
        scratch_shapes=scratch,
        compiler_params=_cparams("arbitrary"),
        name="proj",
    )(x, w_ext, cos, sin)


def _proj_rwkv_kernel(x_ref, w_ref, pr_ref):
    pr_ref[...] = jnp.dot(x_ref[...].astype(BF16), w_ref[:, 3 * HW:P_EXT], preferred_element_type=F32)


def _proj_rwkv(x, w_ext, layer):
    n = x.shape[0]
    return pl.pallas_call(
        _proj_rwkv_kernel,
        grid=(1,),
        in_specs=[pl.BlockSpec((n, D_MODEL), lambda i: (0, 0)),
                  pl.BlockSpec((None, D_MODEL, P_EXT), lambda i: (layer, 0, 0))],
        out_specs=pl.BlockSpec((n, PR_W), lambda i: (0, 0)),
        out_shape=jax.ShapeDtypeStruct((n, PR_W), F32),
        compiler_params=_cparams("arbitrary"),
        name="proj_prev",
    )(x, w_ext)


def _att_kernel(*refs, has_prev, is_last, g, scatter_out, tps):
    q_ref, k_ref, v_ref, kp_ref, vp_ref = refs[:5]
    pos = 5
    st_ref = gain_ref = None
    if has_prev:
        st_ref = refs[pos]
        pos += 1
    if is_last:
        gain_ref = refs[pos]
        pos += 1
    out_ref = refs[pos]
    st_scr = refs[pos + 1] if scatter_out else None

    per = BLK // g
    n = pl.program_id(2)

    def tile(ref, j):
        if g == 1:
            return ref[j * BLK:(j + 1) * BLK, :]
        return ref[:, j * per:(j + 1) * per, :].reshape(BLK, ref.shape[-1])

    qs = [tile(q_ref, j) for j in range(tps)]
    ks = [tile(k_ref, j) for j in range(tps)]
    vs = [tile(v_ref, j) for j in range(tps)]
    kps = [kp_ref[...].reshape(BLK, HW)] + ks[:-1]
    vps = [vp_ref[...].reshape(BLK, HW)] + vs[:-1]
    order = lambda i: (i % per) * g + i // per
    qo = order(lax.broadcasted_iota(jnp.int32, (BLK, BLK), 0))
    ko = order(lax.broadcasted_iota(jnp.int32, (BLK, BLK), 1))
    valid_c = ko <= qo
    valid_p = [jnp.where(ko >= qo, jnp.where(n > 0, 1, 0), 0) > 0] + [ko >= qo] * (tps - 1)
    nt = (((1,), (1,)), ((), ()))
    pairs = [(j, slice(h * HEAD_DIM, (h + 1) * HEAD_DIM)) for j in range(tps) for h in range(N_HEADS)]
    s_c = [jnp.where(valid_c, lax.dot_general(qs[j][:, sl], ks[j][:, sl], nt, preferred_element_type=F32),
                     -jnp.inf) for j, sl in pairs]
    s_p = [jnp.where(valid_p[j], lax.dot_general(qs[j][:, sl], kps[j][:, sl], nt, preferred_element_type=F32),
                     -jnp.inf) for j, sl in pairs]
    m_h = [jnp.maximum(jnp.max(a, axis=-1, keepdims=True), jnp.max(b, axis=-1, keepdims=True))
           for a, b in zip(s_c, s_p)]
    e_c = [jnp.exp(a - mm) for a, mm in zip(s_c, m_h)]
    e_p = [jnp.exp(a - mm) for a, mm in zip(s_p, m_h)]
    l_h = [jnp.sum(a, axis=-1, keepdims=True) + jnp.sum(b, axis=-1, keepdims=True) for a, b in zip(e_c, e_p)]
    pv = [jnp.dot(a.astype(BF16), vs[j][:, sl], preferred_element_type=F32)
          + jnp.dot(b.astype(BF16), vps[j][:, sl], preferred_element_type=F32)
          for a, b, (j, sl) in zip(e_c, e_p, pairs)]
    head_of_lane = lax.broadcasted_iota(jnp.int32, (128, HW), 1) // HEAD_DIM
    sel = jnp.where(lax.broadcasted_iota(jnp.int32, (128, HW), 0) == head_of_lane, 1.0, 0.0).astype(BF16)

    def expand(x):
        hi = x.astype(BF16)
        r1 = x - hi.astype(F32)
        mid = r1.astype(BF16)
        low = (r1 - mid.astype(F32)).astype(BF16)
        d = functools.partial(jnp.dot, preferred_element_type=F32)
        return d(hi, sel) + d(mid, sel) + d(low, sel)

    lane = lax.broadcasted_iota(jnp.int32, (BLK, 128), 1)
    for j in range(tps):
        acc = jnp.concatenate(pv[j * N_HEADS:(j + 1) * N_HEADS], axis=1)
        m = jnp.zeros((BLK, 128), F32)
        l = jnp.ones((BLK, 128), F32)
        for h in range(N_HEADS):
            m = jnp.where(lane == h, m_h[j * N_HEADS + h], m)
            l = jnp.where(lane == h, l_h[j * N_HEADS + h], l)
        if has_prev:
            st = tile(st_ref, j)
            m0 = st[:, HW:HW + 128]
            l0 = st[:, HW + 128:ST_W]
            mn = jnp.maximum(m0, m)
            a0 = jnp.exp(m0 - mn)
            a1 = jnp.exp(m - mn)
            acc = expand(a0) * st[:, 0:HW] + expand(a1) * acc
            l = a0 * l0 + a1 * l
            m = mn
        if is_last:
            att = acc * expand(1.0 / l)
            ms = jnp.mean(att * att, axis=-1, keepdims=True)
            out_ref[j * BLK:(j + 1) * BLK, :] = (att * lax.rsqrt(ms + RMS_EPS) * gain_ref[...]).astype(BF16)
        elif scatter_out:
            stage = st_scr.at[j]
            nrow = BLK // N_CLS
            _stage_lane_tiles(stage, jnp.concatenate([acc, m, l], axis=1))
            for r in range(N_CLS):
                out_ref[r, j * nrow:(j + 1) * nrow, :] = _strided_rows(stage, r, nrow, N_CLS)
        else:
            out_ref[:, j * per:(j + 1) * per, :] = jnp.concatenate([acc, m, l], axis=1).reshape(g, per, ST_W)


def _att_prompt(q, kb, vb, qc, kc, vc, gain):
    B, T, _ = q.shape
    m16 = T // N_CLS
    call = lambda kern, grid, in_specs, out_spec, out_shape, scratch, name: pl.pallas_call(
        kern, grid=grid, in_specs=in_specs, out_specs=out_spec, out_shape=out_shape,
        scratch_shapes=scratch, compiler_params=_cparams("parallel", "parallel", "arbitrary"), name=name)
    tps = ATT_TPS
    prev = lambda n: jnp.maximum(tps * n - 1, 0)
    kern = lambda **kw: functools.partial(_att_kernel, tps=tps, **kw)

    blk = pl.BlockSpec((None, BLK * tps, HW), lambda b, r, n: (b, n, 0))
    pblk = pl.BlockSpec((None, BLK, HW), lambda b, r, n: (b, prev(n), 0))
    st1 = call(kern(has_prev=False, is_last=False, g=1, scatter_out=True),
               (B, 1, T // (BLK * tps)), [blk] * 3 + [pblk] * 2,
               pl.BlockSpec((None, N_CLS, BLK * tps // N_CLS, ST_W), lambda b, r, n: (b, 0, n, 0)),
               jax.ShapeDtypeStruct((B, N_CLS, m16, ST_W), F32),
               [pltpu.VMEM((tps, ST_W // 128, BLK, 128), F32)],
               "att_d1")(q, kb, vb, kb, vb)

    g = N_CLS // 4
    per = BLK // g
    v5 = lambda a: a.reshape(B, g, 4, m16, a.shape[-1])
    blk = lambda w: pl.BlockSpec((None, g, None, per * tps, w), lambda b, r, n: (b, 0, r, n, 0))
    pblk = pl.BlockSpec((None, g, None, per, HW), lambda b, r, n: (b, 0, r, prev(n), 0))
    st4 = call(kern(has_prev=True, is_last=False, g=g, scatter_out=False),
               (B, 4, m16 // (per * tps)), [blk(HW)] * 3 + [pblk] * 2 + [blk(ST_W)], blk(ST_W),
               jax.ShapeDtypeStruct((B, g, 4, m16, ST_W), F32), [],
               "att_d4")(v5(qc), v5(kc), v5(vc), v5(kc), v5(vc), v5(st1))
    st4 = st4.reshape(B, N_CLS, m16, ST_W)

    blk = lambda w: pl.BlockSpec((None, None, BLK * tps, w), lambda b, r, n: (b, r, n, 0))
    pblk = pl.BlockSpec((None, None, BLK, HW), lambda b, r, n: (b, r, prev(n), 0))
    return call(kern(has_prev=True, is_last=True, g=1, scatter_out=False),
                (B, N_CLS, m16 // (BLK * tps)),
                [blk(HW)] * 3 + [pblk] * 2 + [blk(ST_W), pl.BlockSpec((1, HW), lambda b, r, n: (0, 0))],
                blk(HW), jax.ShapeDtypeStruct((B, N_CLS, m16, HW), BF16), [],
                "att_d16")(qc, kc, vc, kc, vc, st4, gain)


def _att_dec_kernel(q_ref, kn_ref, vn_ref, k_ref, v_ref, gain_ref, out_ref):
    L = k_ref.shape[-1]
    dist = L - lax.broadcasted_iota(jnp.int32, (1, L), 1)
    mult = jnp.zeros((1, L), F32)
    for d in DILATIONS:
        mult = mult + jnp.where((dist & (d - 1)) == 0, jnp.where(dist <= d * N_BACK, 1.0, 0.0), 0.0)
    active = mult > 0.0
    q = q_ref[...]
    kn = kn_ref[...]
    vn = vn_ref[...]
    lane = lax.broadcasted_iota(jnp.int32, (HEAD_DIM, N_HEADS), 1)
    o = jnp.zeros((HEAD_DIM, N_HEADS), F32)
    for h in range(N_HEADS):
        qc = q[:, h:h + 1]
        s = jnp.where(active, jnp.sum(k_ref[h] * qc, axis=0, keepdims=True), -jnp.inf)
        s_new = jnp.sum(qc * kn[:, h:h + 1], axis=0, keepdims=True)
        m = jnp.maximum(jnp.max(s, axis=-1, keepdims=True), s_new)
        e = mult * jnp.exp(s - m)
        e_new = jnp.exp(s_new - m) * float(len(DILATIONS))
        l = jnp.sum(e, axis=-1, keepdims=True) + e_new
        acc = jnp.sum(v_ref[h] * e, axis=-1, keepdims=True) + e_new * vn[:, h:h + 1]
        o = jnp.where(lane == h, acc / l, o)
    ms = jnp.sum(jnp.sum(o * o, axis=-1, keepdims=True), axis=0, keepdims=True) * (1.0 / HW)
    out_ref[...] = o * lax.rsqrt(ms + RMS_EPS) * gain_ref[...]


def _att_decode(q, kn, vn, cache_kt, cache_vt, layer, gain):
    bd, L = cache_kt.shape[1], cache_kt.shape[-1]
    row = pl.BlockSpec((None, HEAD_DIM, N_HEADS), lambda b: (b, 0, 0))
    cblk = pl.BlockSpec((None, None, N_HEADS, HEAD_DIM, L), lambda b: (layer, b, 0, 0, 0))
    return pl.pallas_call(
        _att_dec_kernel,
        grid=(bd,),
        in_specs=[row, row, row, cblk, cblk, pl.BlockSpec((HEAD_DIM, N_HEADS), lambda b: (0, 0))],
        out_specs=row,
        out_shape=jax.ShapeDtypeStruct((bd, HEAD_DIM, N_HEADS), F32),
        compiler_params=_cparams("parallel"),
        name="att_dec",
    )(q, kn, vn, cache_kt, cache_vt, gain)


def _softplus(x):
    return jnp.maximum(x, 0.0) + jnp.log(1.0 + jnp.exp(-jnp.abs(x)))


def _rwkv_pre(z, vfirst, prm, has_vres):
    (dbase, up1, ibase, gup, vbase, vup, ksk, ksa, bd) = prm
    r = z[:, 0:HW]
    k = z[:, HW:2 * HW]
    v = z[:, 2 * HW:3 * HW]
    lo1 = z[:, 3 * HW:3 * HW + 128]
    lane = lax.broadcasted_iota(jnp.int32, lo1.shape, 1)
    lo1 = jnp.where(lane < LORA_DECAY, jnp.tanh(lo1), lo1)
    up = _mm(lo1, up1)
    w = -_softplus(-(dbase + up[:, 0:HW])) - 0.5
    logw = -jnp.exp(w)
    iclr = jax.nn.sigmoid(ibase + up[:, HW:2 * HW])
    g = _mm(jax.nn.sigmoid(z[:, 3 * HW + 128:3 * HW + 256]), gup)
    if has_vres:
        mix = jax.nn.sigmoid(vbase + _mm(z[:, 3 * HW + 256:3 * HW + 384], vup))
        v = v + (vfirst - v) * mix
    kk = k * ksk
    nrm = jnp.sqrt(_sum_heads(kk * kk, bd))
    kk = kk / jnp.maximum(nrm, 1e-12)
    k = k * (1.0 + (iclr - 1.0) * ksa)
    return r, k, v, logw, -kk, kk * iclr, g


def _rwkv_post(y, r, k, v, g, brk, gng, gnb, bd):
    mu = _sum_heads(y, bd) * (1.0 / HEAD_DIM)
    yc = y - mu
    var = _sum_heads(yc * yc, bd) * (1.0 / HEAD_DIM)
    yn = yc * lax.rsqrt(var + GN_EPS) * gng + gnb
    bonus = _sum_heads(r * k * brk, bd) * v
    return (yn + bonus) * g


def _bmm(a, b):
    return jnp.einsum('bij,bjk->bik', a.astype(BF16), b.astype(BF16), preferred_element_type=F32)


def _bmm_hi(a, b):
    ah, al = _split2(a)
    bh, bl = _split2(b)
    d = functools.partial(jnp.einsum, 'bij,bjk->bik', preferred_element_type=F32)
    return d(ah, bh) + d(ah, bl) + d(al, bh)


def _rwkv_chunk(at, rt, btT, ktT, vv, pcb, masks):
    strict, incl, eye, levels = masks
    c = CHUNK
    qr = jnp.concatenate([at, rt], axis=1)
    gb = _bmm(qr, btT)
    gk = _bmm(qr, ktT)
    a_ab = jnp.where(strict, gb[:, 0:c], 0.0)
    a_rb = jnp.where(incl, gb[:, c:2 * c], 0.0)
    a_ak = jnp.where(strict, gk[:, 0:c], 0.0)
    a_rk = jnp.where(incl, gk[:, c:2 * c], 0.0)
    t = eye + jnp.where(levels[0], a_ab, 0.0)
    for lv in levels[1:]:
        t = t + _bmm(_bmm(t, jnp.where(lv, a_ab, 0.0)), t)
    xv = _bmm(jnp.concatenate([a_ak, a_rk, ktT], axis=1), vv)
    w = _bmm(t, at)
    z = _bmm(t, xv[:, 0:c])
    lhs = jnp.concatenate([a_rb, btT], axis=1)
    lw = _bmm(lhs, w)
    lz = _bmm(lhs, z)
    rp = rt + lw[:, 0:c]
    y0 = lz[:, 0:c] + xv[:, c:2 * c]
    m = pcb * (eye + lw[:, c:2 * c])
    n = pcb * (lz[:, c:2 * c] + xv[:, 2 * c:3 * c])
    return rp, y0, m, n


def _chunk_masks():
    ri = lax.broadcasted_iota(jnp.int32, (CHUNK, CHUNK), 0)
    ci = lax.broadcasted_iota(jnp.int32, (CHUNK, CHUNK), 1)
    strict = ri > ci
    incl = ri >= ci
    eye = jnp.where(ri == ci, 1.0, 0.0).astype(F32)
    levels = []
    size = 2
    while size <= CHUNK:
        same = (ri // size) == (ci // size)
        if size == 2:
            levels.append(same)
        else:
            inner = (ri // (size // 2)) == (ci // (size // 2))
            levels.append(jnp.where(same, jnp.where(inner, 0, 1), 0) > 0)
        size *= 2
    return strict, incl, eye, levels


def _rwkv_kernel(*refs, has_vres, tb):
    pr_ref = refs[0]
    pos = 1
    vf_ref = None
    if has_vres:
        vf_ref = refs[pos]
        pos += 1
    (mu_ref, dbase, up1, ibase, gup, vbase, vup, ksk, ksa, brk, gng, gnb, bd_ref, tri_ref) = refs[pos:pos + 14]
    pos += 14
    rw_ref = refs[pos]
    pos += 1
    vfo_ref = None
    if not has_vres:
        vfo_ref = refs[pos]
        pos += 1
    so_ref = refs[pos]
    pos += 1
    h_scr, carry, at4, rt4, bt4, kt4, v4, pc4, y4, y_s = refs[pos:pos + 10]
    nc = tb // CHUNK

    i = pl.program_id(1)

    @pl.when(i == 0)
    def _():
        h_scr[...] = jnp.zeros_like(h_scr)
        carry[...] = jnp.zeros_like(carry)

    p = pr_ref[...]
    row = lax.broadcasted_iota(jnp.int32, p.shape, 0)
    prev = jnp.where(row == 0, carry[0:1, :], pltpu.roll(p, 1, 0))
    carry[0:1, :] = p[tb - 1:tb, :]
    z = p + (prev - p) * mu_ref[...]
    bd = bd_ref[...]
    prm = (dbase[...], up1[...], ibase[...], gup[...], vbase[...], vup[...], ksk[...], ksa[...], bd)
    r, k, v, logw, a, b, g = _rwkv_pre(z, vf_ref[...] if has_vres else None, prm, has_vres)
    if not has_vres:
        vfo_ref[...] = v
    hi = logw.astype(BF16)
    r1 = logw - hi.astype(F32)
    mid = r1.astype(BF16)
    low = (r1 - mid.astype(F32)).astype(BF16)
    tri = tri_ref[...]
    d = functools.partial(jnp.dot, preferred_element_type=F32)
    cum = d(tri, hi) + d(tri, mid) + d(tri, low)
    ecum = jnp.exp(cum)
    einv = jnp.exp(-cum)
    a_t = a * jnp.exp(cum - logw)
    r_t = r * ecum
    btT = (b * einv).T
    ktT = (k * einv).T
    pcT = ecum.T
    for c in range(nc):
        rs = slice(c * CHUNK, (c + 1) * CHUNK)
        for h in range(N_HEADS):
            ls = slice(h * HEAD_DIM, (h + 1) * HEAD_DIM)
            at4[c, h] = a_t[rs, ls]
            rt4[c, h] = r_t[rs, ls]
            v4[c, h] = v[rs, ls]
            bt4[c, h] = btT[ls, rs]
            kt4[c, h] = ktT[ls, rs]
            pc4[c, h] = jnp.broadcast_to(pcT[ls, (c + 1) * CHUNK - 1:(c + 1) * CHUNK], (HEAD_DIM, CHUNK))
    masks = _chunk_masks()

    def chunks(i, _):
        c0 = i * RWKV_GROUP
        grp = lambda ref: ref[pl.ds(c0, RWKV_GROUP)].reshape(RWKV_GROUP * N_HEADS, HEAD_DIM, HEAD_DIM)
        rp, y0, m, n = _rwkv_chunk(grp(at4), grp(rt4), grp(bt4), grp(kt4), grp(v4), grp(pc4), masks)
        h = h_scr[...]
        for j in range(RWKV_GROUP):
            hs = slice(j * N_HEADS, (j + 1) * N_HEADS)
            y4[c0 + j] = _bmm(rp[hs], h) + y0[hs]
            h = _bmm_hi(m[hs], h) + n[hs]
        h_scr[...] = h
        return 0

    lax.fori_loop(0, nc // RWKV_GROUP, chunks, 0)
    for c in range(nc):
        for h in range(N_HEADS):
            y_s[c * CHUNK:(c + 1) * CHUNK, h * HEAD_DIM:(h + 1) * HEAD_DIM] = y4[c, h]
    rw = _rwkv_post(y_s[...], r, k, v, g, brk[...], gng[...], gnb[...], bd)
    rw_ref[...] = rw.astype(BF16)

    @pl.when(i == pl.num_programs(1) - 1)
    def _():
        hst = h_scr[...]
        hi = hst.astype(BF16)
        r1 = hst - hi.astype(F32)
        mid = r1.astype(BF16)
        low = (r1 - mid.astype(F32)).astype(BF16)
        eye = jnp.broadcast_to(masks[2].astype(BF16)[None], (N_HEADS, HEAD_DIM, HEAD_DIM))
        d = functools.partial(jnp.einsum, 'bik,bjk->bij', preferred_element_type=F32)
        so_ref[...] = d(eye, hi) + d(eye, mid) + d(eye, low)


def _rwkv_prompt(pr, vfirst, wts, tb):
    B, T, _ = pr.shape
    has_vres = vfirst is not None
    blk = lambda w: pl.BlockSpec((None, tb, w), lambda b, i: (b, i, 0))
    const = lambda a: pl.BlockSpec(a.shape, lambda b, i: (0,) * a.ndim)
    args, specs = [pr], [blk(PR_W)]
    if has_vres:
        args.append(vfirst)
        specs.append(blk(HW))
    idx = jnp.arange(tb)
    tri = ((idx[:, None] // CHUNK == idx[None, :] // CHUNK) & (idx[:, None] >= idx[None, :])).astype(BF16)
    for a in wts + (tri,):
        args.append(a)
        specs.append(const(a))
    out_shape = [jax.ShapeDtypeStruct((B, T, HW), BF16)]
    out_specs = [blk(HW)]
    if not has_vres:
        out_shape.append(jax.ShapeDtypeStruct((B, T, HW), F32))
        out_specs.append(blk(HW))
    out_shape.append(jax.ShapeDtypeStruct((B, N_HEADS, HEAD_DIM, HEAD_DIM), F32))
    out_specs.append(pl.BlockSpec((None, N_HEADS, HEAD_DIM, HEAD_DIM), lambda b, i: (b, 0, 0, 0)))
    per_head = lambda: pltpu.VMEM((tb // CHUNK, N_HEADS, HEAD_DIM, HEAD_DIM), F32)
    scratch = [pltpu.VMEM((N_HEADS, HEAD_DIM, HEAD_DIM), F32), pltpu.VMEM((8, PR_W), F32)]
    scratch += [per_head() for _ in range(7)] + [pltpu.VMEM((tb, HW), F32)]
    outs = pl.pallas_call(
        functools.partial(_rwkv_kernel, has_vres=has_vres, tb=tb),
        grid=(B, T // tb),
        in_specs=specs,
        out_specs=out_specs,
        out_shape=out_shape,
        scratch_shapes=scratch,
        compiler_params=_cparams("parallel", "arbitrary"),
        name="rwkv",
    )(*args)
    if has_vres:
        return outs[0], vfirst, outs[1]
    return outs[0], outs[1], outs[2]


def _rwkv_dec_kernel(*refs, has_vres):
    pr_ref, pv_ref = refs[0:2]
    pos = 2
    vf_ref = None
    if has_vres:
        vf_ref = refs[pos]
        pos += 1
    s_ref = refs[pos]
    pos += 1
    (mu_ref, dbase, up1, ibase, gup, vbase, vup, ksk, ksa, brk, gng, gnb, bd_ref) = refs[pos:pos + 13]
    pos += 13
    rw_ref = refs[pos]
    pos += 1
    vfo_ref = None
    if not has_vres:
        vfo_ref = refs[pos]
        pos += 1
    so_ref, y_s = refs[pos:pos + 2]

    p = pr_ref[...]
    z = p + (pv_ref[...] - p) * mu_ref[...]
    bd = bd_ref[...]
    prm = (dbase[...], up1[...], ibase[...], gup[...], vbase[...], vup[...], ksk[...], ksa[...], bd)
    r, k, v, logw, a, b, g = _rwkv_pre(z, vf_ref[...] if has_vres else None, prm, has_vres)
    if not has_vres:
        vfo_ref[...] = v
    w = jnp.exp(logw)
    ri = lax.broadcasted_iota(jnp.int32, (HEAD_DIM, HEAD_DIM), 0)
    ci = lax.broadcasted_iota(jnp.int32, (HEAD_DIM, HEAD_DIM), 1)
    eye = jnp.where(ri == ci, 1.0, 0.0).astype(F32)
    for i0 in range(0, p.shape[0], 2):
        idx = [(i, h, slice(i, i + 1), slice(h * HEAD_DIM, (h + 1) * HEAD_DIM))
               for i in range(i0, i0 + 2) for h in range(N_HEADS)]
        s0 = [s_ref[i, h] for i, h, _, _ in idx]
        sa = [jnp.sum(s * a[row, sl], axis=-1, keepdims=True) for s, (_, _, row, sl) in zip(s0, idx)]
        vcol = [jnp.sum(eye * v[row, sl], axis=-1, keepdims=True) for _, _, row, sl in idx]
        s1 = [s * w[row, sl] + x * b[row, sl] + vc * k[row, sl]
              for s, x, vc, (_, _, row, sl) in zip(s0, sa, vcol, idx)]
        for s, (i, h, _, _) in zip(s1, idx):
            so_ref[i, h] = s
        ycol = [jnp.sum(s * r[row, sl], axis=-1, keepdims=True) for s, (_, _, row, sl) in zip(s1, idx)]
        for yc, (_, _, row, sl) in zip(ycol, idx):
            y_s[row, sl] = jnp.sum(eye * yc, axis=0, keepdims=True)
    rw_ref[...] = _rwkv_post(y_s[...], r, k, v, g, brk[...], gng[...], gnb[...], bd)


DEC_ROWS = 8


def _rwkv_decode(pr, prev, vfirst, state, layer, wts):
    bd_ = pr.shape[0]
    rows = DEC_ROWS
    has_vres = vfirst is not None
    row = lambda w: pl.BlockSpec((rows, w), lambda b: (b, 0))
    const = lambda a: pl.BlockSpec(a.shape, lambda b: (0,) * a.ndim)
    sblk = (rows, N_HEADS, HEAD_DIM, HEAD_DIM)
    args, specs = [pr, prev], [row(PR_W), row(PR_W)]
    if has_vres:
        args.append(vfirst)
        specs.append(row(HW))
    args.append(state)
    specs.append(pl.BlockSpec((None,) + sblk, lambda b: (layer, b, 0, 0, 0)))
    for a in wts:
        args.append(a)
        specs.append(const(a))
    out_shape = [jax.ShapeDtypeStruct((bd_, HW), F32)]
    out_specs = [row(HW)]
    if not has_vres:
        out_shape.append(jax.ShapeDtypeStruct((bd_, HW), F32))
        out_specs.append(row(HW))
    out_shape.append(jax.ShapeDtypeStruct((bd_, N_HEADS, HEAD_DIM, HEAD_DIM), F32))
    out_specs.append(pl.BlockSpec(sblk, lambda b: (b, 0, 0, 0)))
    outs = pl.pallas_call(
        functools.partial(_rwkv_dec_kernel, has_vres=has_vres),
        grid=(bd_ // rows,),
        in_specs=specs,
        out_specs=out_specs,
        out_shape=out_shape,
        scratch_shapes=[pltpu.VMEM((rows, HW), F32)],
        compiler_params=_cparams("parallel"),
        name="rwkv_dec",
    )(*args)
    if has_vres:
        return outs[0], vfirst, outs[1]
    return outs[0], outs[1], outs[2]


def _mix_ffn_kernel(att_ref, rw_ref, x_ref, wo_ref, g1_ref, b1_ref, wu_ref, wd_ref, g2_ref, b2_ref,
                    out_ref, *scr):
    d = functools.partial(jnp.dot, preferred_element_type=F32)
    if scr:
        nat, = scr
        rows = att_ref.shape[1]
        for r in range(N_CLS):
            blk = att_ref[r].astype(F32)
            for j in range(nat.shape[0]):
                nat[j, pl.ds(r, rows, stride=N_CLS), :] = blk[:, j * 128:(j + 1) * 128]
        att = jnp.concatenate([nat[j] for j in range(nat.shape[0])], axis=1).astype(BF16)
    else:
        att = att_ref[...]
    mix = d(att, wo_ref[0:HW, :]) + d(rw_ref[...], wo_ref[HW:2 * HW, :])
    x1 = _layer_norm(ALPHA * x_ref[...] + mix, g1_ref[...], b1_ref[...])
    xb = x1.astype(BF16)
    acc = ALPHA * x1
    for j in range(D_FF // FF_CHUNK):
        cs = slice(j * FF_CHUNK, (j + 1) * FF_CHUNK)
        h = jnp.maximum(d(xb, wu_ref[:, cs]), 0.0)
        acc = acc + d((h * h).astype(BF16), wd_ref[cs, :])
    out_ref[...] = _layer_norm(acc, g2_ref[...], b2_ref[...])


FF_CHUNK = 1024


def _mix_ffn(att, rw, x, w_out, w_up, w_down, layer, ln, tm):
    n = x.shape[0]
    row = lambda w: pl.BlockSpec((tm, w), lambda i: (i, 0))
    vec = pl.BlockSpec((None, 1, D_MODEL), lambda i: (layer, 0, 0))
    wgt = lambda r, c: pl.BlockSpec((None, r, c), lambda i: (layer, 0, 0))
    scratch = []
    att_spec = row(HW)
    if att.ndim == 4:
        tpb = att.shape[2] * N_CLS // tm
        att_spec = pl.BlockSpec((None, N_CLS, tm // N_CLS, HW), lambda i: (i // tpb, 0, i % tpb, 0))
        scratch = [pltpu.VMEM((HW // 128, tm, 128), F32)]
    return pl.pallas_call(
        _mix_ffn_kernel,
        grid=(n // tm,),
        in_specs=[att_spec, row(HW), row(D_MODEL), wgt(D_MODEL, D_MODEL), vec, vec,
                  wgt(D_MODEL, D_FF), wgt(D_FF, D_MODEL), vec, vec],
        out_specs=row(D_MODEL),
        out_shape=jax.ShapeDtypeStruct((n, D_MODEL), F32),
        scratch_shapes=scratch,
        compiler_params=_cparams("parallel"),
        name="mix_ffn",
    )(att, rw, x, w_out, ln[0], ln[1], w_up, w_down, ln[2], ln[3])


def _rope_tables(pos):
    half = HEAD_DIM // 2
    inv = ROPE_THETA ** (-jnp.arange(half, dtype=F32) * (2.0 / HEAD_DIM))
    ang = pos.astype(F32)[:, None] * inv[None, :]
    cos, sin = jnp.cos(ang), jnp.sin(ang)
    cos2 = jnp.concatenate([cos, cos, cos, cos], axis=1)
    sin2 = jnp.concatenate([-sin, sin, -sin, sin], axis=1)
    return cos2, sin2


def kernel(x_prompt, x_sample, state_shift, state_wkv, cache_k, cache_v, w_in, w_vres_in, shift_mu, vres_mu, decay_base, decay_up, iclr_base, iclr_up, gate_up, vres_base, vres_up, key_scale_k, key_scale_a, bonus_rk, gn_g, gn_b, att_gain, w_out, ln1_g, ln1_b, w_ff_up, w_ff_down, ln2_g, ln2_b):
    B, T, _ = x_prompt.shape
    bd_, ds_, _ = x_sample.shape
    L = cache_k.shape[2]
    past_len = 8192
    assert ds_ == 1 and T % (N_CLS * BLK * ATT_TPS) == 0
    keep = min(DILATIONS[-1] * N_BACK, T)
    tm_p, tb = 512, 256

    z32 = jnp.zeros((1, D_MODEL, LORA_VRES), F32)
    w_ext = jnp.concatenate([w_in, jnp.concatenate([z32, w_vres_in], 0),
                             jnp.zeros((DEPTH, D_MODEL, LO_W - 288), F32)], axis=-1).astype(BF16)
    mu_ext = jnp.concatenate([shift_mu, jnp.concatenate([jnp.zeros((1, LORA_VRES), F32), vres_mu], 0),
                              jnp.zeros((DEPTH, LO_W - 288), F32)], axis=-1)
    zdu = jnp.zeros((DEPTH, LORA_DECAY, HW), F32)
    up1 = jnp.concatenate([jnp.concatenate([decay_up, zdu], -1), jnp.concatenate([zdu, iclr_up], -1)],
                          axis=1).astype(BF16)
    gup = gate_up.astype(BF16)
    vup = jnp.concatenate([jnp.concatenate([jnp.zeros((1, LORA_VRES, HW), F32), vres_up], 0),
                           jnp.zeros((DEPTH, 128 - LORA_VRES, HW), F32)], axis=1).astype(BF16)
    vbase = jnp.concatenate([jnp.zeros((1, HW), F32), vres_base], 0)
    hid = jnp.arange(HW) // HEAD_DIM
    bd = (hid[:, None] == hid[None, :]).astype(BF16)
    w_out_b = w_out.astype(BF16)
    w_up_b = w_ff_up.astype(BF16)
    w_down_b = w_ff_down.astype(BF16)
    r3 = lambda a: a.reshape(DEPTH, 1, a.shape[-1])
    ln = (r3(ln1_g), r3(ln1_b), r3(ln2_g), r3(ln2_b))

    cos_p, sin_p = _rope_tables(jnp.arange(T))
    cos_s, sin_s = _rope_tables(jnp.full((bd_,), past_len))
    cache_kt = jnp.transpose(cache_k, (0, 1, 3, 4, 2))
    cache_vt = jnp.transpose(cache_v, (0, 1, 3, 4, 2))

    hp = x_prompt.reshape(B * T, D_MODEL)
    hs = x_sample.reshape(bd_, D_MODEL)
    vf_p = vf_s = None
    shift_p, shift_s, wkv_p, wkv_s, kp_l, vp_l, ks_l, vs_l = [], [], [], [], [], [], [], []
    for l in range(DEPTH):
        row1 = lambda a: a[l].reshape(1, -1)
        wts = (row1(mu_ext), row1(decay_base), up1[l], row1(iclr_base), gup[l], row1(vbase), vup[l],
               row1(key_scale_k), row1(key_scale_a), row1(bonus_rk), row1(gn_g), row1(gn_b), bd)
        gain = row1(att_gain)
        shift_p.append(hp.reshape(B, T, D_MODEL)[:, -1])
        shift_s.append(hs)

        q, kb, vb, kf, vf, pr, qc, kc, vc = _proj(hp, w_ext, l, cos_p, sin_p, tm_p, seq_len=T, keep=keep)
        r3d = lambda a: a.reshape(B, T, a.shape[-1])
        att = _att_prompt(r3d(q), r3d(kb), r3d(vb), qc, kc, vc, gain)
        rw, vf_p, s_p = _rwkv_prompt(r3d(pr), vf_p, wts, tb)
        hp = _mix_ffn(att, rw.reshape(B * T, HW), hp, w_out_b, w_up_b, w_down_b, l, ln, 256)
        wkv_p.append(s_p)
        kp_l.append(kf.reshape(B, keep, N_HEADS, HEAD_DIM))
        vp_l.append(vf.reshape(B, keep, N_HEADS, HEAD_DIM))

        qs, kbs, vbs, kfs, vfs, prs = _proj(hs, w_ext, l, cos_s, sin_s, bd_)
        prev = _proj_rwkv(state_shift[l], w_ext, l)
        e1 = lambda a: a.reshape(bd_, 1, a.shape[-1])
        hd = lambda a: a.astype(F32).reshape(-1, N_HEADS, HEAD_DIM).swapaxes(-1, -2)
        att_s = _att_decode(hd(qs), hd(kfs), hd(vfs), cache_kt, cache_vt, l, hd(gain)[0])
        att_s = att_s.swapaxes(-1, -2)
        rw_s, vf_s, s_s = _rwkv_decode(prs, prev, vf_s, state_wkv, l, wts)
        hs = _mix_ffn(att_s.reshape(bd_, HW).astype(BF16), rw_s.astype(BF16), hs, w_out_b, w_up_b, w_down_b,
                      l, ln, bd_)
        wkv_s.append(s_s)
        ks_l.append(kfs.reshape(bd_, 1, N_HEADS, HEAD_DIM))
        vs_l.append(vfs.reshape(bd_, 1, N_HEADS, HEAD_DIM))

    return (hp.reshape(B, T, D_MODEL), hs.reshape(bd_, 1, D_MODEL), jnp.stack(shift_p), jnp.stack(shift_s),
            jnp.stack(wkv_p), jnp.stack(wkv_s), jnp.stack(kp_l), jnp.stack(vp_l), jnp.stack(ks_l),
            jnp.stack(vs_l))
```

```python
import functools

import jax
import jax.numpy as jnp
from jax import lax
from jax.experimental import pallas as pl
from jax.experimental.pallas import tpu as pltpu

F32 = jnp.float32
BF16 = jnp.bfloat16

D_MODEL = 1024
DEPTH = 4
HEAD_DIM = 64
N_HEADS = 8
HW = N_HEADS * HEAD_DIM
LORA_DECAY, LORA_ICLR, LORA_GATE, LORA_VRES = 64, 64, 128, 32
LO_W = 384
PR_W = 3 * HW + LO_W
P_EXT = 3 * HW + PR_W
DILATIONS = (1, 4, 16)
N_CLS = 16
N_BACK = 128
BLK = 128
ST_W = HW + 256
ROPE_THETA = 10000.0
D_FF = 4 * D_MODEL
ALPHA = (2.0 * DEPTH) ** 0.25
LN_EPS = 1e-5
GN_EPS = 64e-5
RMS_EPS = 1e-6
CHUNK = 64
RWKV_GROUP = 4
ATT_TPS = 4
VMEM_LIMIT = 56 * 1024 * 1024


def _cparams(*sem):
    return pltpu.CompilerParams(dimension_semantics=sem, vmem_limit_bytes=VMEM_LIMIT)


def _mm(a, b):
    return jnp.dot(a.astype(BF16), b.astype(BF16), preferred_element_type=F32)


def _split2(x):
    hi = x.astype(BF16)
    lo = (x - hi.astype(F32)).astype(BF16)
    return hi, lo


def _sum_heads(x, bd):
    hi, lo = _split2(x)
    d = functools.partial(jnp.dot, preferred_element_type=F32)
    return d(hi, bd) + d(lo, bd)


def _stage_lane_tiles(scr, val):
    for j in range(scr.shape[0]):
        scr[j] = val[:, j * 128:(j + 1) * 128]


def _strided_rows(scr, start, rows, stride):
    return jnp.concatenate([scr[j, pl.ds(start, rows, stride=stride), :] for j in range(scr.shape[0])], axis=1)


def _layer_norm(x, g, b):
    mu = jnp.mean(x, axis=-1, keepdims=True)
    xc = x - mu
    var = jnp.mean(xc * xc, axis=-1, keepdims=True)
    return xc * lax.rsqrt(var + LN_EPS) * g + b


def _rope(p, cos, sin):
    lane = lax.broadcasted_iota(jnp.int32, p.shape, 1) % HEAD_DIM
    rot = jnp.where(lane < HEAD_DIM // 2, pltpu.roll(p, HW - HEAD_DIM // 2, 1),
                    pltpu.roll(p, HEAD_DIM // 2, 1))
    return p * cos + rot * sin


def _proj_kernel(x_ref, w_ref, cos_ref, sin_ref, q_ref, kb_ref, vb_ref, kf_ref, vf_ref, pr_ref, *cm):
    xb = x_ref[...].astype(BF16)
    cos = jnp.concatenate([cos_ref[...]] * 4, axis=1)
    sin = jnp.concatenate([sin_ref[...]] * 4, axis=1)
    d = functools.partial(jnp.dot, preferred_element_type=F32)

    def class_major(val, dst_ref, scr):
        _stage_lane_tiles(scr, val)
        rows = val.shape[0] // N_CLS
        for r in range(N_CLS):
            dst_ref[r] = _strided_rows(scr, r, rows, N_CLS).astype(BF16)

    q = _rope(d(xb, w_ref[:, 0:HW]), cos, sin) * (HEAD_DIM ** -0.5)
    q_ref[...] = q.astype(BF16)
    k = _rope(d(xb, w_ref[:, HW:2 * HW]), cos, sin)
    kf_ref[...] = k
    kb_ref[...] = k.astype(BF16)
    v = d(xb, w_ref[:, 2 * HW:3 * HW])
    vf_ref[...] = v
    vb_ref[...] = v.astype(BF16)
    if cm:
        qc_ref, kc_ref, vc_ref, scr = cm
        class_major(q, qc_ref, scr)
        class_major(k, kc_ref, scr)
        class_major(v, vc_ref, scr)
    pr_ref[...] = d(xb, w_ref[:, 3 * HW:P_EXT])


def _proj(x, w_ext, layer, cos, sin, tm, seq_len=None, keep=None):
    n = x.shape[0]
    t_tiles = cos.shape[0] // tm
    row = lambda i: (i, 0)
    kv_spec, kv_rows = pl.BlockSpec((tm, HW), row), n
    if seq_len is not None:
        tpb, ktiles = seq_len // tm, keep // tm
        kv_spec = pl.BlockSpec(
            (tm, HW), lambda i: ((i // tpb) * ktiles + jnp.maximum(i % tpb - (tpb - ktiles), 0), 0))
        kv_rows = n // seq_len * keep
    out_specs = [pl.BlockSpec((tm, HW), row)] * 3 + [kv_spec] * 2 + [pl.BlockSpec((tm, PR_W), row)]
    out_shape = ([jax.ShapeDtypeStruct((n, HW), BF16)] * 3 + [jax.ShapeDtypeStruct((kv_rows, HW), F32)] * 2
                 + [jax.ShapeDtypeStruct((n, PR_W), F32)])
    scratch = []
    if seq_len is not None:
        tpb = seq_len // tm
        cm_blk = pl.BlockSpec((None, N_CLS, tm // N_CLS, HW), lambda i: (i // tpb, 0, i % tpb, 0))
        out_specs += [cm_blk] * 3
        out_shape += [jax.ShapeDtypeStruct((n // seq_len, N_CLS, seq_len // N_CLS, HW), BF16)] * 3
        scratch = [pltpu.VMEM((HW // 128, tm, 128), F32)]
    return pl.pallas_call(
        _proj_kernel,
        grid=(n // tm,),
        in_specs=[pl.BlockSpec((tm, D_MODEL), row),
                  pl.BlockSpec((None, D_MODEL, P_EXT), lambda i: (layer, 0, 0)),
                  pl.BlockSpec((tm, 128), lambda i: (i % t_tiles, 0)),
                  pl.BlockSpec((tm, 128), lambda i: (i % t_tiles, 0))],
        out_specs=out_specs,
        out_shape=out_shape,
        scratch_shapes=scratch,
        compiler_params=_cparams("arbitrary"),
        name="proj",
    )(x, w_ext, cos, sin)


def _proj_rwkv_kernel(x_ref, w_ref, pr_ref):
    pr_ref[...] = jnp.dot(x_ref[...].astype(BF16), w_ref[:, 3 * HW:P_EXT], preferred_element_type=F32)


def _proj_rwkv(x, w_ext, layer):
    n = x.shape[0]
    return pl.pallas_call(
        _proj_rwkv_kernel,
        grid=(1,),
        in_specs=[pl.BlockSpec((n, D_MODEL), lambda i: (0, 0)),
                  pl.BlockSpec((None, D_MODEL, P_EXT), lambda i: (layer, 0, 0))],
        out_specs=pl.BlockSpec((n, PR_W), lambda i: (0, 0)),
        out_shape=jax.ShapeDtypeStruct((n, PR_W), F32),
        compiler_params=_cparams("arbitrary"),
        name="proj_prev",
    )(x, w_ext)


def _att_kernel(*refs, has_prev, is_last, g, scatter_out, tps):
    q_ref, k_ref, v_ref, kp_ref, vp_ref = refs[:5]
    pos = 5
    st_ref = gain_ref = None
    if has_prev:
        st_ref = refs[pos]
        pos += 1
    if is_last:
        gain_ref = refs[pos]
        pos += 1
    out_ref = refs[pos]
    st_scr = refs[pos + 1] if scatter_out else None

    per = BLK // g
    n = pl.program_id(2)

    def tile(ref, j):
        if g == 1:
            return ref[j * BLK:(j + 1) * BLK, :]
        return ref[:, j * per:(j + 1) * per, :].reshape(BLK, ref.shape[-1])

    qs = [tile(q_ref, j) for j in range(tps)]
    ks = [tile(k_ref, j) for j in range(tps)]
    vs = [tile(v_ref, j) for j in range(tps)]
    kps = [kp_ref[...].reshape(BLK, HW)] + ks[:-1]
    vps = [vp_ref[...].reshape(BLK, HW)] + vs[:-1]
    order = lambda i: (i % per) * g + i // per
    qo = order(lax.broadcasted_iota(jnp.int32, (BLK, BLK), 0))
    ko = order(lax.broadcasted_iota(jnp.int32, (BLK, BLK), 1))
    valid_c = ko <= qo
    valid_p = [jnp.where(ko >= qo, jnp.where(n > 0, 1, 0), 0) > 0] + [ko >= qo] * (tps - 1)
    nt = (((1,), (1,)), ((), ()))
    pairs = [(j, slice(h * HEAD_DIM, (h + 1) * HEAD_DIM)) for j in range(tps) for h in range(N_HEADS)]
    s_c = [jnp.where(valid_c, lax.dot_general(qs[j][:, sl], ks[j][:, sl], nt, preferred_element_type=F32),
                     -jnp.inf) for j, sl in pairs]
    s_p = [jnp.where(valid_p[j], lax.dot_general(qs[j][:, sl], kps[j][:, sl], nt, preferred_element_type=F32),
                     -jnp.inf) for j, sl in pairs]
    m_h = [jnp.maximum(jnp.max(a, axis=-1, keepdims=True), jnp.max(b, axis=-1, keepdims=True))
           for a, b in zip(s_c, s_p)]
    e_c = [jnp.exp(a - mm) for a, mm in zip(s_c, m_h)]
    e_p = [jnp.exp(a - mm) for a, mm in zip(s_p, m_h)]
    l_h = [jnp.sum(a, axis=-1, keepdims=True) + jnp.sum(b, axis=-1, keepdims=True) for a, b in zip(e_c, e_p)]
    pv = [jnp.dot(a.astype(BF16), vs[j][:, sl], preferred_element_type=F32)
          + jnp.dot(b.astype(BF16), vps[j][:, sl], preferred_element_type=F32)
          for a, b, (j, sl) in zip(e_c, e_p, pairs)]
    head_of_lane = lax.broadcasted_iota(jnp.int32, (128, HW), 1) // HEAD_DIM
    sel = jnp.where(lax.broadcasted_iota(jnp.int32, (128, HW), 0) == head_of_lane, 1.0, 0.0).astype(BF16)

    def expand(x):
        hi = x.astype(BF16)
        r1 = x - hi.astype(F32)
        mid = r1.astype(BF16)
        low = (r1 - mid.astype(F32)).astype(BF16)
        d = functools.partial(jnp.dot, preferred_element_type=F32)
        return d(hi, sel) + d(mid, sel) + d(low, sel)

    lane = lax.broadcasted_iota(jnp.int32, (BLK, 128), 1)
    for j in range(tps):
        acc = jnp.concatenate(pv[j * N_HEADS:(j + 1) * N_HEADS], axis=1)
        m = jnp.zeros((BLK, 128), F32)
        l = jnp.ones((BLK, 128), F32)
        for h in range(N_HEADS):
            m = jnp.where(lane == h, m_h[j * N_HEADS + h], m)
            l = jnp.where(lane == h, l_h[j * N_HEADS + h], l)
        if has_prev:
            st = tile(st_ref, j)
            m0 = st[:, HW:HW + 128]
            l0 = st[:, HW + 128:ST_W]
            mn = jnp.maximum(m0, m)
            a0 = jnp.exp(m0 - mn)
            a1 = jnp.exp(m - mn)
            acc = expand(a0) * st[:, 0:HW] + expand(a1) * acc
            l = a0 * l0 + a1 * l
            m = mn
        if is_last:
            att = acc * expand(1.0 / l)
            ms = jnp.mean(att * att, axis=-1, keepdims=True)
            out_ref[j * BLK:(j + 1) * BLK, :] = (att * lax.rsqrt(ms + RMS_EPS) * gain_ref[...]).astype(BF16)
        elif scatter_out:
            stage = st_scr.at[j]
            nrow = BLK // N_CLS
            _stage_lane_tiles(stage, jnp.concatenate([acc, m, l], axis=1))
            for r in range(N_CLS):
                out_ref[r, j * nrow:(j + 1) * nrow, :] = _strided_rows(stage, r, nrow, N_CLS)
        else:
            out_ref[:, j * per:(j + 1) * per, :] = jnp.concatenate([acc, m, l], axis=1).reshape(g, per, ST_W)


def _att_prompt(q, kb, vb, qc, kc, vc, gain):
    B, T, _ = q.shape
    m16 = T // N_CLS
    call = lambda kern, grid, in_specs, out_spec, out_shape, scratch, name: pl.pallas_call(
        kern, grid=grid, in_specs=in_specs, out_specs=out_spec, out_shape=out_shape,
        scratch_shapes=scratch, compiler_params=_cparams("parallel", "parallel", "arbitrary"), name=name)
    tps = ATT_TPS
    prev = lambda n: jnp.maximum(tps * n - 1, 0)
    kern = lambda **kw: functools.partial(_att_kernel, tps=tps, **kw)

    blk = pl.BlockSpec((None, BLK * tps, HW), lambda b, r, n: (b, n, 0))
    pblk = pl.BlockSpec((None, BLK, HW), lambda b, r, n: (b, prev(n), 0))
    st1 = call(kern(has_prev=False, is_last=False, g=1, scatter_out=True),
               (B, 1, T // (BLK * tps)), [blk] * 3 + [pblk] * 2,
               pl.BlockSpec((None, N_CLS, BLK * tps // N_CLS, ST_W), lambda b, r, n: (b, 0, n, 0)),
               jax.ShapeDtypeStruct((B, N_CLS, m16, ST_W), F32),
               [pltpu.VMEM((tps, ST_W // 128, BLK, 128), F32)],
               "att_d1")(q, kb, vb, kb, vb)

    g = N_CLS // 4
    per = BLK // g
    v5 = lambda a: a.reshape(B, g, 4, m16, a.shape[-1])
    blk = lambda w: pl.BlockSpec((None, g, None, per * tps, w), lambda b, r, n: (b, 0, r, n, 0))
    pblk = pl.BlockSpec((None, g, None, per, HW), lambda b, r, n: (b, 0, r, prev(n), 0))
    st4 = call(kern(has_prev=True, is_last=False, g=g, scatter_out=False),
               (B, 4, m16 // (per * tps)), [blk(HW)] * 3 + [pblk] * 2 + [blk(ST_W)], blk(ST_W),
               jax.ShapeDtypeStruct((B, g, 4, m16, ST_W), F32), [],
               "att_d4")(v5(qc), v5(kc), v5(vc), v5(kc), v5(vc), v5(st1))
    st4 = st4.reshape(B, N_CLS, m16, ST_W)

    blk = lambda w: pl.BlockSpec((None, None, BLK * tps, w), lambda b, r, n: (b, r, n, 0))
    pblk = pl.BlockSpec((None, None, BLK, HW), lambda b, r, n: (b, r, prev(n), 0))
    return call(kern(has_prev=True, is_last=True, g=1, scatter_out=False),
                (B, N_CLS, m16 // (BLK * tps)),
                [blk(HW)] * 3 + [pblk] * 2 + [blk(ST_W), pl.BlockSpec((1, HW), lambda b, r, n: (0, 0))],
                blk(HW), jax.ShapeDtypeStruct((B, N_CLS, m16, HW), BF16), [],
                "att_d16")(qc, kc, vc, kc, vc, st4, gain)


def _att_dec_kernel(q_ref, kn_ref, vn_ref, k_ref, v_ref, gain_ref, out_ref):
    L = k_ref.shape[-1]
    dist = L - lax.broadcasted_iota(jnp.int32, (1, L), 1)
    mult = jnp.zeros((1, L), F32)
    for d in DILATIONS:
        mult = mult + jnp.where((dist & (d - 1)) == 0, jnp.where(dist <= d * N_BACK, 1.0, 0.0), 0.0)
    active = mult > 0.0
    q = q_ref[...]
    kn = kn_ref[...]
    vn = vn_ref[...]
    lane = lax.broadcasted_iota(jnp.int32, (HEAD_DIM, N_HEADS), 1)
    o = jnp.zeros((HEAD_DIM, N_HEADS), F32)
    for h in range(N_HEADS):
        qc = q[:, h:h + 1]
        s = jnp.where(active, jnp.sum(k_ref[h] * qc, axis=0, keepdims=True), -jnp.inf)
        s_new = jnp.sum(qc * kn[:, h:h + 1], axis=0, keepdims=True)
        m = jnp.maximum(jnp.max(s, axis=-1, keepdims=True), s_new)
        e = mult * jnp.exp(s - m)
        e_new = jnp.exp(s_new - m) * float(len(DILATIONS))
        l = jnp.sum(e, axis=-1, keepdims=True) + e_new
        acc = jnp.sum(v_ref[h] * e, axis=-1, keepdims=True) + e_new * vn[:, h:h + 1]
        o = jnp.where(lane == h, acc / l, o)
    ms = jnp.sum(jnp.sum(o * o, axis=-1, keepdims=True), axis=0, keepdims=True) * (1.0 / HW)
    out_ref[...] = o * lax.rsqrt(ms + RMS_EPS) * gain_ref[...]


def _att_decode(q, kn, vn, cache_kt, cache_vt, layer, gain):
    bd, L = cache_kt.shape[1], cache_kt.shape[-1]
    row = pl.BlockSpec((None, HEAD_DIM, N_HEADS), lambda b: (b, 0, 0))
    cblk = pl.BlockSpec((None, None, N_HEADS, HEAD_DIM, L), lambda b: (layer, b, 0, 0, 0))
    return pl.pallas_call(
        _att_dec_kernel,
        grid=(bd,),
        in_specs=[row, row, row, cblk, cblk, pl.BlockSpec((HEAD_DIM, N_HEADS), lambda b: (0, 0))],
        out_specs=row,
        out_shape=jax.ShapeDtypeStruct((bd, HEAD_DIM, N_HEADS), F32),
        compiler_params=_cparams("parallel"),
        name="att_dec",
    )(q, kn, vn, cache_kt, cache_vt, gain)


def _softplus(x):
    return jnp.maximum(x, 0.0) + jnp.log(1.0 + jnp.exp(-jnp.abs(x)))


def _rwkv_pre(z, vfirst, prm, has_vres):
    (dbase, up1, ibase, gup, vbase, vup, ksk, ksa, bd) = prm
    r = z[:, 0:HW]
    k = z[:, HW:2 * HW]
    v = z[:, 2 * HW:3 * HW]
    lo1 = z[:, 3 * HW:3 * HW + 128]
    lane = lax.broadcasted_iota(jnp.int32, lo1.shape, 1)
    lo1 = jnp.where(lane < LORA_DECAY, jnp.tanh(lo1), lo1)
    up = _mm(lo1, up1)
    w = -_softplus(-(dbase + up[:, 0:HW])) - 0.5
    logw = -jnp.exp(w)
    iclr = jax.nn.sigmoid(ibase + up[:, HW:2 * HW])
    g = _mm(jax.nn.sigmoid(z[:, 3 * HW + 128:3 * HW + 256]), gup)
    if has_vres:
        mix = jax.nn.sigmoid(vbase + _mm(z[:, 3 * HW + 256:3 * HW + 384], vup))
        v = v + (vfirst - v) * mix
    kk = k * ksk
    nrm = jnp.sqrt(_sum_heads(kk * kk, bd))
    kk = kk / jnp.maximum(nrm, 1e-12)
    k = k * (1.0 + (iclr - 1.0) * ksa)
    return r, k, v, logw, -kk, kk * iclr, g


def _rwkv_post(y, r, k, v, g, brk, gng, gnb, bd):
    mu = _sum_heads(y, bd) * (1.0 / HEAD_DIM)
    yc = y - mu
    var = _sum_heads(yc * yc, bd) * (1.0 / HEAD_DIM)
    yn = yc * lax.rsqrt(var + GN_EPS) * gng + gnb
    bonus = _sum_heads(r * k * brk, bd) * v
    return (yn + bonus) * g


def _bmm(a, b):
    return jnp.einsum('bij,bjk->bik', a.astype(BF16), b.astype(BF16), preferred_element_type=F32)


def _bmm_hi(a, b):
    ah, al = _split2(a)
    bh, bl = _split2(b)
    d = functools.partial(jnp.einsum, 'bij,bjk->bik', preferred_element_type=F32)
    return d(ah, bh) + d(ah, bl) + d(al, bh)


def _rwkv_chunk(at, rt, btT, ktT, vv, pcb, masks):
    strict, incl, eye, levels = masks
    c = CHUNK
    qr = jnp.concatenate([at, rt], axis=1)
    gb = _bmm(qr, btT)
    gk = _bmm(qr, ktT)
    a_ab = jnp.where(strict, gb[:, 0:c], 0.0)
    a_rb = jnp.where(incl, gb[:, c:2 * c], 0.0)
    a_ak = jnp.where(strict, gk[:, 0:c], 0.0)
    a_rk = jnp.where(incl, gk[:, c:2 * c], 0.0)
    t = eye + jnp.where(levels[0], a_ab, 0.0)
    for lv in levels[1:]:
        t = t + _bmm(_bmm(t, jnp.where(lv, a_ab, 0.0)), t)
    xv = _bmm(jnp.concatenate([a_ak, a_rk, ktT], axis=1), vv)
    w = _bmm(t, at)
    z = _bmm(t, xv[:, 0:c])
    lhs = jnp.concatenate([a_rb, btT], axis=1)
    lw = _bmm(lhs, w)
    lz = _bmm(lhs, z)
    rp = rt + lw[:, 0:c]
    y0 = lz[:, 0:c] + xv[:, c:2 * c]
    m = pcb * (eye + lw[:, c:2 * c])
    n = pcb * (lz[:, c:2 * c] + xv[:, 2 * c:3 * c])
    return rp, y0, m, n


def _chunk_masks():
    ri = lax.broadcasted_iota(jnp.int32, (CHUNK, CHUNK), 0)
    ci = lax.broadcasted_iota(jnp.int32, (CHUNK, CHUNK), 1)
    strict = ri > ci
    incl = ri >= ci
    eye = jnp.where(ri == ci, 1.0, 0.0).astype(F32)
    levels = []
    size = 2
    while size <= CHUNK:
        same = (ri // size) == (ci // size)
        if size == 2:
            levels.append(same)
        else:
            inner = (ri // (size // 2)) == (ci // (size // 2))
            levels.append(jnp.where(same, jnp.where(inner, 0, 1), 0) > 0)
        size *= 2
    return strict, incl, eye, levels


def _rwkv_kernel(*refs, has_vres, tb):
    pr_ref = refs[0]
    pos = 1
    vf_ref = None
    if has_vres:
        vf_ref = refs[pos]
        pos += 1
    (mu_ref, dbase, up1, ibase, gup, vbase, vup, ksk, ksa, brk, gng, gnb, bd_ref, tri_ref) = refs[pos:pos + 14]
    pos += 14
    rw_ref = refs[pos]
    pos += 1
    vfo_ref = None
    if not has_vres:
        vfo_ref = refs[pos]
        pos += 1
    so_ref = refs[pos]
    pos += 1
    h_scr, carry, at4, rt4, bt4, kt4, v4, pc4, y4, y_s = refs[pos:pos + 10]
    nc = tb // CHUNK

    i = pl.program_id(1)

    @pl.when(i == 0)
    def _():
        h_scr[...] = jnp.zeros_like(h_scr)
        carry[...] = jnp.zeros_like(carry)

    p = pr_ref[...]
    row = lax.broadcasted_iota(jnp.int32, p.shape, 0)
    prev = jnp.where(row == 0, carry[0:1, :], pltpu.roll(p, 1, 0))
    carry[0:1, :] = p[tb - 1:tb, :]
    z = p + (prev - p) * mu_ref[...]
    bd = bd_ref[...]
    prm = (dbase[...], up1[...], ibase[...], gup[...], vbase[...], vup[...], ksk[...], ksa[...], bd)
    r, k, v, logw, a, b, g = _rwkv_pre(z, vf_ref[...] if has_vres else None, prm, has_vres)
    if not has_vres:
        vfo_ref[...] = v
    hi = logw.astype(BF16)
    r1 = logw - hi.astype(F32)
    mid = r1.astype(BF16)
    low = (r1 - mid.astype(F32)).astype(BF16)
    tri = tri_ref[...]
    d = functools.partial(jnp.dot, preferred_element_type=F32)
    cum = d(tri, hi) + d(tri, mid) + d(tri, low)
    ecum = jnp.exp(cum)
    einv = jnp.exp(-cum)
    a_t = a * jnp.exp(cum - logw)
    r_t = r * ecum
    btT = (b * einv).T
    ktT = (k * einv).T
    pcT = ecum.T
    for c in range(nc):
        rs = slice(c * CHUNK, (c + 1) * CHUNK)
        for h in range(N_HEADS):
            ls = slice(h * HEAD_DIM, (h + 1) * HEAD_DIM)
            at4[c, h] = a_t[rs, ls]
            rt4[c, h] = r_t[rs, ls]
            v4[c, h] = v[rs, ls]
            bt4[c, h] = btT[ls, rs]
            kt4[c, h] = ktT[ls, rs]
            pc4[c, h] = jnp.broadcast_to(pcT[ls, (c + 1) * CHUNK - 1:(c + 1) * CHUNK], (HEAD_DIM, CHUNK))
    masks = _chunk_masks()

    def chunks(i, _):
        c0 = i * RWKV_GROUP
        grp = lambda ref: ref[pl.ds(c0, RWKV_GROUP)].reshape(RWKV_GROUP * N_HEADS, HEAD_DIM, HEAD_DIM)
        rp, y0, m, n = _rwkv_chunk(grp(at4), grp(rt4), grp(bt4), grp(kt4), grp(v4), grp(pc4), masks)
        h = h_scr[...]
        for j in range(RWKV_GROUP):
            hs = slice(j * N_HEADS, (j + 1) * N_HEADS)
            y4[c0 + j] = _bmm(rp[hs], h) + y0[hs]
            h = _bmm_hi(m[hs], h) + n[hs]
        h_scr[...] = h
        return 0

    lax.fori_loop(0, nc // RWKV_GROUP, chunks, 0)
    for c in range(nc):
        for h in range(N_HEADS):
            y_s[c * CHUNK:(c + 1) * CHUNK, h * HEAD_DIM:(h + 1) * HEAD_DIM] = y4[c, h]
    rw = _rwkv_post(y_s[...], r, k, v, g, brk[...], gng[...], gnb[...], bd)
    rw_ref[...] = rw.astype(BF16)

    @pl.when(i == pl.num_programs(1) - 1)
    def _():
        hst = h_scr[...]
        hi = hst.astype(BF16)
        r1 = hst - hi.astype(F32)
        mid = r1.astype(BF16)
        low = (r1 - mid.astype(F32)).astype(BF16)
        eye = jnp.broadcast_to(masks[2].astype(BF16)[None], (N_HEADS, HEAD_DIM, HEAD_DIM))
        d = functools.partial(jnp.einsum, 'bik,bjk->bij', preferred_element_type=F32)
        so_ref[...] = d(eye, hi) + d(eye, mid) + d(eye, low)


def _rwkv_prompt(pr, vfirst, wts, tb):
    B, T, _ = pr.shape
    has_vres = vfirst is not None
    blk = lambda w: pl.BlockSpec((None, tb, w), lambda b, i: (b, i, 0))
    const = lambda a: pl.BlockSpec(a.shape, lambda b, i: (0,) * a.ndim)
    args, specs = [pr], [blk(PR_W)]
    if has_vres:
        args.append(vfirst)
        specs.append(blk(HW))
    idx = jnp.arange(tb)
    tri = ((idx[:, None] // CHUNK == idx[None, :] // CHUNK) & (idx[:, None] >= idx[None, :])).astype(BF16)
    for a in wts + (tri,):
        args.append(a)
        specs.append(const(a))
    out_shape = [jax.ShapeDtypeStruct((B, T, HW), BF16)]
    out_specs = [blk(HW)]
    if not has_vres:
        out_shape.append(jax.ShapeDtypeStruct((B, T, HW), F32))
        out_specs.append(blk(HW))
    out_shape.append(jax.ShapeDtypeStruct((B, N_HEADS, HEAD_DIM, HEAD_DIM), F32))
    out_specs.append(pl.BlockSpec((None, N_HEADS, HEAD_DIM, HEAD_DIM), lambda b, i: (b, 0, 0, 0)))
    per_head = lambda: pltpu.VMEM((tb // CHUNK, N_HEADS, HEAD_DIM, HEAD_DIM), F32)
    scratch = [pltpu.VMEM((N_HEADS, HEAD_DIM, HEAD_DIM), F32), pltpu.VMEM((8, PR_W), F32)]
    scratch += [per_head() for _ in range(7)] + [pltpu.VMEM((tb, HW), F32)]
    outs = pl.pallas_call(
        functools.partial(_rwkv_kernel, has_vres=has_vres, tb=tb),
        grid=(B, T // tb),
        in_specs=specs,
        out_specs=out_specs,
        out_shape=out_shape,
        scratch_shapes=scratch,
        compiler_params=_cparams("parallel", "arbitrary"),
        name="rwkv",
    )(*args)
    if has_vres:
        return outs[0], vfirst, outs[1]
    return outs[0], outs[1], outs[2]


def _rwkv_dec_kernel(*refs, has_vres):
    pr_ref, pv_ref = refs[0:2]
    pos = 2
    vf_ref = None
    if has_vres:
        vf_ref = refs[pos]
        pos += 1
    s_ref = refs[pos]
    pos += 1
    (mu_ref, dbase, up1, ibase, gup, vbase, vup, ksk, ksa, brk, gng, gnb, bd_ref) = refs[pos:pos + 13]
    pos += 13
    rw_ref = refs[pos]
    pos += 1
    vfo_ref = None
    if not has_vres:
        vfo_ref = refs[pos]
        pos += 1
    so_ref, y_s = refs[pos:pos + 2]

    p = pr_ref[...]
    z = p + (pv_ref[...] - p) * mu_ref[...]
    bd = bd_ref[...]
    prm = (dbase[...], up1[...], ibase[...], gup[...], vbase[...], vup[...], ksk[...], ksa[...], bd)
    r, k, v, logw, a, b, g = _rwkv_pre(z, vf_ref[...] if has_vres else None, prm, has_vres)
    if not has_vres:
        vfo_ref[...] = v
    w = jnp.exp(logw)
    ri = lax.broadcasted_iota(jnp.int32, (HEAD_DIM, HEAD_DIM), 0)
    ci = lax.broadcasted_iota(jnp.int32, (HEAD_DIM, HEAD_DIM), 1)
    eye = jnp.where(ri == ci, 1.0, 0.0).astype(F32)
    for i0 in range(0, p.shape[0], 2):
        idx = [(i, h, slice(i, i + 1), slice(h * HEAD_DIM, (h + 1) * HEAD_DIM))
               for i in range(i0, i0 + 2) for h in range(N_HEADS)]
        s0 = [s_ref[i, h] for i, h, _, _ in idx]
        sa = [jnp.sum(s * a[row, sl], axis=-1, keepdims=True) for s, (_, _, row, sl) in zip(s0, idx)]
        vcol = [jnp.sum(eye * v[row, sl], axis=-1, keepdims=True) for _, _, row, sl in idx]
        s1 = [s * w[row, sl] + x * b[row, sl] + vc * k[row, sl]
              for s, x, vc, (_, _, row, sl) in zip(s0, sa, vcol, idx)]
        for s, (i, h, _, _) in zip(s1, idx):
            so_ref[i, h] = s
        ycol = [jnp.sum(s * r[row, sl], axis=-1, keepdims=True) for s, (_, _, row, sl) in zip(s1, idx)]
        for yc, (_, _, row, sl) in zip(ycol, idx):
            y_s[row, sl] = jnp.sum(eye * yc, axis=0, keepdims=True)
    rw_ref[...] = _rwkv_post(y_s[...], r, k, v, g, brk[...], gng[...], gnb[...], bd)


DEC_ROWS = 8


def _rwkv_decode(pr, prev, vfirst, state, layer, wts):
    bd_ = pr.shape[0]
    rows = DEC_ROWS
    has_vres = vfirst is not None
    row = lambda w: pl.BlockSpec((rows, w), lambda b: (b, 0))
    const = lambda a: pl.BlockSpec(a.shape, lambda b: (0,) * a.ndim)
    sblk = (rows, N_HEADS, HEAD_DIM, HEAD_DIM)
    args, specs = [pr, prev], [row(PR_W), row(PR_W)]
    if has_vres:
        args.append(vfirst)
        specs.append(row(HW))
    args.append(state)
    specs.append(pl.BlockSpec((None,) + sblk, lambda b: (layer, b, 0, 0, 0)))
    for a in wts:
        args.append(a)
        specs.append(const(a))
    out_shape = [jax.ShapeDtypeStruct((bd_, HW), F32)]
    out_specs = [row(HW)]
    if not has_vres:
        out_shape.append(jax.ShapeDtypeStruct((bd_, HW), F32))
        out_specs.append(row(HW))
    out_shape.append(jax.ShapeDtypeStruct((bd_, N_HEADS, HEAD_DIM, HEAD_DIM), F32))
    out_specs.append(pl.BlockSpec(sblk, lambda b: (b, 0, 0, 0)))
    outs = pl.pallas_call(
        functools.partial(_rwkv_dec_kernel, has_vres=has_vres),
        grid=(bd_ // rows,),
        in_specs=specs,
        out_specs=out_specs,
        out_shape=out_shape,
        scratch_shapes=[pltpu.VMEM((rows, HW), F32)],
        compiler_params=_cparams("parallel"),
        name="rwkv_dec",
    )(*args)
    if has_vres:
        return outs[0], vfirst, outs[1]
    return outs[0], outs[1], outs[2]


def _mix_ffn_kernel(att_ref, rw_ref, x_ref, wo_ref, g1_ref, b1_ref, wu_ref, wd_ref, g2_ref, b2_ref,
                    out_ref, *scr):
    d = functools.partial(jnp.dot, preferred_element_type=F32)
    if scr:
        nat, = scr
        rows = att_ref.shape[1]
        for r in range(N_CLS):
            blk = att_ref[r].astype(F32)
            for j in range(nat.shape[0]):
                nat[j, pl.ds(r, rows, stride=N_CLS), :] = blk[:, j * 128:(j + 1) * 128]
        att = jnp.concatenate([nat[j] for j in range(nat.shape[0])], axis=1).astype(BF16)
    else:
        att = att_ref[...]
    mix = d(att, wo_ref[0:HW, :]) + d(rw_ref[...], wo_ref[HW:2 * HW, :])
    x1 = _layer_norm(ALPHA * x_ref[...] + mix, g1_ref[...], b1_ref[...])
    xb = x1.astype(BF16)
    acc = ALPHA * x1
    for j in range(D_FF // FF_CHUNK):
        cs = slice(j * FF_CHUNK, (j + 1) * FF_CHUNK)
        h = jnp.maximum(d(xb, wu_ref[:, cs]), 0.0)
        acc = acc + d((h * h).astype(BF16), wd_ref[cs, :])
    out_ref[...] = _layer_norm(acc, g2_ref[...], b2_ref[...])


FF_CHUNK = 1024


def _mix_ffn(att, rw, x, w_out, w_up, w_down, layer, ln, tm):
    n = x.shape[0]
    row = lambda w: pl.BlockSpec((tm, w), lambda i: (i, 0))
    vec = pl.BlockSpec((None, 1, D_MODEL), lambda i: (layer, 0, 0))
    wgt = lambda r, c: pl.BlockSpec((None, r, c), lambda i: (layer, 0, 0), pipeline_mode=pl.Buffered(1))
    scratch = []
    att_spec = row(HW)
    if att.ndim == 4:
        tpb = att.shape[2] * N_CLS // tm
        att_spec = pl.BlockSpec((None, N_CLS, tm // N_CLS, HW), lambda i: (i // tpb, 0, i % tpb, 0))
        scratch = [pltpu.VMEM((HW // 128, tm, 128), F32)]
    return pl.pallas_call(
        _mix_ffn_kernel,
        grid=(n // tm,),
        in_specs=[att_spec, row(HW), row(D_MODEL), wgt(D_MODEL, D_MODEL), vec, vec,
                  wgt(D_MODEL, D_FF), wgt(D_FF, D_MODEL), vec, vec],
        out_specs=row(D_MODEL),
        out_shape=jax.ShapeDtypeStruct((n, D_MODEL), F32),
        scratch_shapes=scratch,
        compiler_params=_cparams("parallel"),
        name="mix_ffn",
    )(att, rw, x, w_out, ln[0], ln[1], w_up, w_down, ln[2], ln[3])


def _rope_tables(pos):
    half = HEAD_DIM // 2
    inv = ROPE_THETA ** (-jnp.arange(half, dtype=F32) * (2.0 / HEAD_DIM))
    ang = pos.astype(F32)[:, None] * inv[None, :]
    cos, sin = jnp.cos(ang), jnp.sin(ang)
    cos2 = jnp.concatenate([cos, cos, cos, cos], axis=1)
    sin2 = jnp.concatenate([-sin, sin, -sin, sin], axis=1)
    return cos2, sin2


def kernel(x_prompt, x_sample, state_shift, state_wkv, cache_k, cache_v, w_in, w_vres_in, shift_mu, vres_mu, decay_base, decay_up, iclr_base, iclr_up, gate_up, vres_base, vres_up, key_scale_k, key_scale_a, bonus_rk, gn_g, gn_b, att_gain, w_out, ln1_g, ln1_b, w_ff_up, w_ff_down, ln2_g, ln2_b):
    B, T, _ = x_prompt.shape
    bd_, ds_, _ = x_sample.shape
    L = cache_k.shape[2]
    past_len = 8192
    assert ds_ == 1 and T % (N_CLS * BLK * ATT_TPS) == 0
    keep = min(DILATIONS[-1] * N_BACK, T)
    tm_p, tb = 512, 256

    z32 = jnp.zeros((1, D_MODEL, LORA_VRES), F32)
    w_ext = jnp.concatenate([w_in, jnp.concatenate([z32, w_vres_in], 0),
                             jnp.zeros((DEPTH, D_MODEL, LO_W - 288), F32)], axis=-1).astype(BF16)
    mu_ext = jnp.concatenate([shift_mu, jnp.concatenate([jnp.zeros((1, LORA_VRES), F32), vres_mu], 0),
                              jnp.zeros((DEPTH, LO_W - 288), F32)], axis=-1)
    zdu = jnp.zeros((DEPTH, LORA_DECAY, HW), F32)
    up1 = jnp.concatenate([jnp.concatenate([decay_up, zdu], -1), jnp.concatenate([zdu, iclr_up], -1)],
                          axis=1).astype(BF16)
    gup = gate_up.astype(BF16)
    vup = jnp.concatenate([jnp.concatenate([jnp.zeros((1, LORA_VRES, HW), F32), vres_up], 0),
                           jnp.zeros((DEPTH, 128 - LORA_VRES, HW), F32)], axis=1).astype(BF16)
    vbase = jnp.concatenate([jnp.zeros((1, HW), F32), vres_base], 0)
    hid = jnp.arange(HW) // HEAD_DIM
    bd = (hid[:, None] == hid[None, :]).astype(BF16)
    w_out_b = w_out.astype(BF16)
    w_up_b = w_ff_up.astype(BF16)
    w_down_b = w_ff_down.astype(BF16)
    r3 = lambda a: a.reshape(DEPTH, 1, a.shape[-1])
    ln = (r3(ln1_g), r3(ln1_b), r3(ln2_g), r3(ln2_b))

    cos_p, sin_p = _rope_tables(jnp.arange(T))
    cos_s, sin_s = _rope_tables(jnp.full((bd_,), past_len))
    cache_kt = jnp.transpose(cache_k, (0, 1, 3, 4, 2))
    cache_vt = jnp.transpose(cache_v, (0, 1, 3, 4, 2))

    hp = x_prompt.reshape(B * T, D_MODEL)
    hs = x_sample.reshape(bd_, D_MODEL)
    vf_p = vf_s = None
    shift_p, shift_s, wkv_p, wkv_s, kp_l, vp_l, ks_l, vs_l = [], [], [], [], [], [], [], []
    for l in range(DEPTH):
        row1 = lambda a: a[l].reshape(1, -1)
        wts = (row1(mu_ext), row1(decay_base), up1[l], row1(iclr_base), gup[l], row1(vbase), vup[l],
               row1(key_scale_k), row1(key_scale_a), row1(bonus_rk), row1(gn_g), row1(gn_b), bd)
        gain = row1(att_gain)
        shift_p.append(hp.reshape(B, T, D_MODEL)[:, -1])
        shift_s.append(hs)

        q, kb, vb, kf, vf, pr, qc, kc, vc = _proj(hp, w_ext, l, cos_p, sin_p, tm_p, seq_len=T, keep=keep)
        r3d = lambda a: a.reshape(B, T, a.shape[-1])
        att = _att_prompt(r3d(q), r3d(kb), r3d(vb), qc, kc, vc, gain)
        rw, vf_p, s_p = _rwkv_prompt(r3d(pr), vf_p, wts, tb)
        hp = _mix_ffn(att, rw.reshape(B * T, HW), hp, w_out_b, w_up_b, w_down_b, l, ln, 512)
        wkv_p.append(s_p)
        kp_l.append(kf.reshape(B, keep, N_HEADS, HEAD_DIM))
        vp_l.append(vf.reshape(B, keep, N_HEADS, HEAD_DIM))

        qs, kbs, vbs, kfs, vfs, prs = _proj(hs, w_ext, l, cos_s, sin_s, bd_)
        prev = _proj_rwkv(state_shift[l], w_ext, l)
        e1 = lambda a: a.reshape(bd_, 1, a.shape[-1])
        hd = lambda a: a.astype(F32).reshape(-1, N_HEADS, HEAD_DIM).swapaxes(-1, -2)
        att_s = _att_decode(hd(qs), hd(kfs), hd(vfs), cache_kt, cache_vt, l, hd(gain)[0])
        att_s = att_s.swapaxes(-1, -2)
        rw_s, vf_s, s_s = _rwkv_decode(prs, prev, vf_s, state_wkv, l, wts)
        hs = _mix_ffn(att_s.reshape(bd_, HW).astype(BF16), rw_s.astype(BF16), hs, w_out_b, w_up_b, w_down_b,
                      l, ln, bd_)
        wkv_s.append(s_s)
        ks_l.append(kfs.reshape(bd_, 1, N_HEADS, HEAD_DIM))
        vs_l.append(vfs.reshape(bd_, 1, N_HEADS, HEAD_DIM))

    return (hp.reshape(B, T, D_MODEL), hs.reshape(bd_, 1, D_MODEL), jnp.stack(shift_p), jnp.stack(shift_s),
            jnp.stack(wkv_p), jnp.stack(wkv_s), jnp.stack(kp_l), jnp.stack(vp_l), jnp.stack(ks_l),
            jnp.stack(vs_l))
```

```python
import functools

import jax
import jax.numpy as jnp
from jax import lax
from jax.experimental import pallas as pl
from jax.experimental.pallas import tpu as pltpu

F32 = jnp.float32
BF16 = jnp.bfloat16

D_MODEL = 1024
DEPTH = 4
HEAD_DIM = 64
N_HEADS = 8
HW = N_HEADS * HEAD_DIM
LORA_DECAY, LORA_ICLR, LORA_GATE, LORA_VRES = 64, 64, 128, 32
LO_W = 384
PR_W = 3 * HW + LO_W
P_EXT = 3 * HW + PR_W
DILATIONS = (1, 4, 16)
N_CLS = 16
N_BACK = 128
BLK = 128
ST_W = HW + 256
ROPE_THETA = 10000.0
D_FF = 4 * D_MODEL
ALPHA = (2.0 * DEPTH) ** 0.25
LN_EPS = 1e-5
GN_EPS = 64e-5
RMS_EPS = 1e-6
CHUNK = 64
RWKV_GROUP = 4
ATT_TPS = (8, 8, 4)
VMEM_LIMIT = 56 * 1024 * 1024


def _cparams(*sem):
    return pltpu.CompilerParams(dimension_semantics=sem, vmem_limit_bytes=VMEM_LIMIT)


def _mm(a, b):
    return jnp.dot(a.astype(BF16), b.astype(BF16), preferred_element_type=F32)


def _split2(x):
    hi = x.astype(BF16)
    lo = (x - hi.astype(F32)).astype(BF16)
    return hi, lo


def _sum_heads(x, bd):
    hi, lo = _split2(x)
    d = functools.partial(jnp.dot, preferred_element_type=F32)
    return d(hi, bd) + d(lo, bd)


def _stage_lane_tiles(scr, val):
    for j in range(scr.shape[0]):
        scr[j] = val[:, j * 128:(j + 1) * 128]


def _strided_rows(scr, start, rows, stride):
    return jnp.concatenate([scr[j, pl.ds(start, rows, stride=stride), :] for j in range(scr.shape[0])], axis=1)


def _layer_norm(x, g, b):
    mu = jnp.mean(x, axis=-1, keepdims=True)
    xc = x - mu
    var = jnp.mean(xc * xc, axis=-1, keepdims=True)
    return xc * lax.rsqrt(var + LN_EPS) * g + b


def _rope(p, cos, sin):
    lane = lax.broadcasted_iota(jnp.int32, p.shape, 1) % HEAD_DIM
    rot = jnp.where(lane < HEAD_DIM // 2, pltpu.roll(p, HW - HEAD_DIM // 2, 1),
                    pltpu.roll(p, HEAD_DIM // 2, 1))
    return p * cos + rot * sin


def _proj_kernel(x_ref, w_ref, cos_ref, sin_ref, q_ref, kb_ref, vb_ref, kf_ref, vf_ref, pr_ref, *cm):
    xb = x_ref[...].astype(BF16)
    cos = jnp.concatenate([cos_ref[...]] * 4, axis=1)
    sin = jnp.concatenate([sin_ref[...]] * 4, axis=1)
    d = functools.partial(jnp.dot, preferred_element_type=F32)

    def class_major(val, dst_ref, scr):
        _stage_lane_tiles(scr, val)
        rows = val.shape[0] // N_CLS
        for r in range(N_CLS):
            dst_ref[r] = _strided_rows(scr, r, rows, N_CLS).astype(BF16)

    q = _rope(d(xb, w_ref[:, 0:HW]), cos, sin) * (HEAD_DIM ** -0.5)
    q_ref[...] = q.astype(BF16)
    k = _rope(d(xb, w_ref[:, HW:2 * HW]), cos, sin)
    kf_ref[...] = k
    kb_ref[...] = k.astype(BF16)
    v = d(xb, w_ref[:, 2 * HW:3 * HW])
    vf_ref[...] = v
    vb_ref[...] = v.astype(BF16)
    if cm:
        qc_ref, kc_ref, vc_ref, scr = cm
        class_major(q, qc_ref, scr)
        class_major(k, kc_ref, scr)
        class_major(v, vc_ref, scr)
    pr_ref[...] = d(xb, w_ref[:, 3 * HW:P_EXT])


def _proj(x, w_ext, layer, cos, sin, tm, seq_len=None, keep=None):
    n = x.shape[0]
    t_tiles = cos.shape[0] // tm
    row = lambda i: (i, 0)
    kv_spec, kv_rows = pl.BlockSpec((tm, HW), row), n
    if seq_len is not None:
        tpb, ktiles = seq_len // tm, keep // tm
        kv_spec = pl.BlockSpec(
            (tm, HW), lambda i: ((i // tpb) * ktiles + jnp.maximum(i % tpb - (tpb - ktiles), 0), 0))
        kv_rows = n // seq_len * keep
    out_specs = [pl.BlockSpec((tm, HW), row)] * 3 + [kv_spec] * 2 + [pl.BlockSpec((tm, PR_W), row)]
    out_shape = ([jax.ShapeDtypeStruct((n, HW), BF16)] * 3 + [jax.ShapeDtypeStruct((kv_rows, HW), F32)] * 2
                 + [jax.ShapeDtypeStruct((n, PR_W), F32)])
    scratch = []
    if seq_len is not None:
        tpb = seq_len // tm
        cm_blk = pl.BlockSpec((None, N_CLS, tm // N_CLS, HW), lambda i: (i // tpb, 0, i % tpb, 0))
        out_specs += [cm_blk] * 3
        out_shape += [jax.ShapeDtypeStruct((n // seq_len, N_CLS, seq_len // N_CLS, HW), BF16)] * 3
        scratch = [pltpu.VMEM((HW // 128, tm, 128), F32)]
    return pl.pallas_call(
        _proj_kernel,
        grid=(n // tm,),
        in_specs=[pl.BlockSpec((tm, D_MODEL), row),
                  pl.BlockSpec((None, D_MODEL, P_EXT), lambda i: (layer, 0, 0)),
                  pl.BlockSpec((tm, 128), lambda i: (i % t_tiles, 0)),
                  pl.BlockSpec((tm, 128), lambda i: (i % t_tiles, 0))],
        out_specs=out_specs,
        out_shape=out_shape,
        scratch_shapes=scratch,
        compiler_params=_cparams("arbitrary"),
        name="proj",
    )(x, w_ext, cos, sin)


def _proj_rwkv_kernel(x_ref, w_ref, pr_ref):
    pr_ref[...] = jnp.dot(x_ref[...].astype(BF16), w_ref[:, 3 * HW:P_EXT], preferred_element_type=F32)


def _proj_rwkv(x, w_ext, layer):
    n = x.shape[0]
    return pl.pallas_call(
        _proj_rwkv_kernel,
        grid=(1,),
        in_specs=[pl.BlockSpec((n, D_MODEL), lambda i: (0, 0)),
                  pl.BlockSpec((None, D_MODEL, P_EXT), lambda i: (layer, 0, 0))],
        out_specs=pl.BlockSpec((n, PR_W), lambda i: (0, 0)),
        out_shape=jax.ShapeDtypeStruct((n, PR_W), F32),
        compiler_params=_cparams("arbitrary"),
        name="proj_prev",
    )(x, w_ext)


def _att_kernel(*refs, has_prev, is_last, g, scatter_out, tps):
    q_ref, k_ref, v_ref, kp_ref, vp_ref = refs[:5]
    pos = 5
    st_ref = gain_ref = None
    if has_prev:
        st_ref = refs[pos]
        pos += 1
    if is_last:
        gain_ref = refs[pos]
        pos += 1
    out_ref = refs[pos]
    st_scr = refs[pos + 1] if scatter_out else None

    per = BLK // g
    n = pl.program_id(2)

    def tile(ref, j):
        if g == 1:
            return ref[j * BLK:(j + 1) * BLK, :]
        return ref[:, j * per:(j + 1) * per, :].reshape(BLK, ref.shape[-1])

    qs = [tile(q_ref, j) for j in range(tps)]
    ks = [tile(k_ref, j) for j in range(tps)]
    vs = [tile(v_ref, j) for j in range(tps)]
    kps = [kp_ref[...].reshape(BLK, HW)] + ks[:-1]
    vps = [vp_ref[...].reshape(BLK, HW)] + vs[:-1]
    order = lambda i: (i % per) * g + i // per
    qo = order(lax.broadcasted_iota(jnp.int32, (BLK, BLK), 0))
    ko = order(lax.broadcasted_iota(jnp.int32, (BLK, BLK), 1))
    valid_c = ko <= qo
    valid_p = [jnp.where(ko >= qo, jnp.where(n > 0, 1, 0), 0) > 0] + [ko >= qo] * (tps - 1)
    nt = (((1,), (1,)), ((), ()))
    pairs = [(j, slice(h * HEAD_DIM, (h + 1) * HEAD_DIM)) for j in range(tps) for h in range(N_HEADS)]
    s_c = [jnp.where(valid_c, lax.dot_general(qs[j][:, sl], ks[j][:, sl], nt, preferred_element_type=F32),
                     -jnp.inf) for j, sl in pairs]
    s_p = [jnp.where(valid_p[j], lax.dot_general(qs[j][:, sl], kps[j][:, sl], nt, preferred_element_type=F32),
                     -jnp.inf) for j, sl in pairs]
    m_h = [jnp.maximum(jnp.max(a, axis=-1, keepdims=True), jnp.max(b, axis=-1, keepdims=True))
           for a, b in zip(s_c, s_p)]
    e_c = [jnp.exp(a - mm) for a, mm in zip(s_c, m_h)]
    e_p = [jnp.exp(a - mm) for a, mm in zip(s_p, m_h)]
    l_h = [jnp.sum(a, axis=-1, keepdims=True) + jnp.sum(b, axis=-1, keepdims=True) for a, b in zip(e_c, e_p)]
    pv = [jnp.dot(a.astype(BF16), vs[j][:, sl], preferred_element_type=F32)
          + jnp.dot(b.astype(BF16), vps[j][:, sl], preferred_element_type=F32)
          for a, b, (j, sl) in zip(e_c, e_p, pairs)]
    head_of_lane = lax.broadcasted_iota(jnp.int32, (128, HW), 1) // HEAD_DIM
    sel = jnp.where(lax.broadcasted_iota(jnp.int32, (128, HW), 0) == head_of_lane, 1.0, 0.0).astype(BF16)

    def expand(x):
        hi = x.astype(BF16)
        r1 = x - hi.astype(F32)
        mid = r1.astype(BF16)
        low = (r1 - mid.astype(F32)).astype(BF16)
        d = functools.partial(jnp.dot, preferred_element_type=F32)
        return d(hi, sel) + d(mid, sel) + d(low, sel)

    lane = lax.broadcasted_iota(jnp.int32, (BLK, 128), 1)
    for j in range(tps):
        acc = jnp.concatenate(pv[j * N_HEADS:(j + 1) * N_HEADS], axis=1)
        m = jnp.zeros((BLK, 128), F32)
        l = jnp.ones((BLK, 128), F32)
        for h in range(N_HEADS):
            m = jnp.where(lane == h, m_h[j * N_HEADS + h], m)
            l = jnp.where(lane == h, l_h[j * N_HEADS + h], l)
        if has_prev:
            st = tile(st_ref, j)
            m0 = st[:, HW:HW + 128]
            l0 = st[:, HW + 128:ST_W]
            mn = jnp.maximum(m0, m)
            a0 = jnp.exp(m0 - mn)
            a1 = jnp.exp(m - mn)
            acc = expand(a0) * st[:, 0:HW] + expand(a1) * acc
            l = a0 * l0 + a1 * l
            m = mn
        if is_last:
            att = acc * expand(1.0 / l)
            ms = jnp.mean(att * att, axis=-1, keepdims=True)
            out_ref[j * BLK:(j + 1) * BLK, :] = (att * lax.rsqrt(ms + RMS_EPS) * gain_ref[...]).astype(BF16)
        elif scatter_out:
            stage = st_scr.at[j]
            nrow = BLK // N_CLS
            _stage_lane_tiles(stage, jnp.concatenate([acc, m, l], axis=1))
            for r in range(N_CLS):
                out_ref[r, j * nrow:(j + 1) * nrow, :] = _strided_rows(stage, r, nrow, N_CLS)
        else:
            out_ref[:, j * per:(j + 1) * per, :] = jnp.concatenate([acc, m, l], axis=1).reshape(g, per, ST_W)


def _att_prompt(q, kb, vb, qc, kc, vc, gain):
    B, T, _ = q.shape
    m16 = T // N_CLS
    call = lambda kern, grid, in_specs, out_spec, out_shape, scratch, name: pl.pallas_call(
        kern, grid=grid, in_specs=in_specs, out_specs=out_spec, out_shape=out_shape,
        scratch_shapes=scratch, compiler_params=_cparams("parallel", "parallel", "arbitrary"), name=name)
    prev = lambda n: jnp.maximum(tps * n - 1, 0)
    kern = lambda **kw: functools.partial(_att_kernel, tps=tps, **kw)

    tps = ATT_TPS[0]
    blk = pl.BlockSpec((None, BLK * tps, HW), lambda b, r, n: (b, n, 0))
    pblk = pl.BlockSpec((None, BLK, HW), lambda b, r, n: (b, prev(n), 0))
    st1 = call(kern(has_prev=False, is_last=False, g=1, scatter_out=True),
               (B, 1, T // (BLK * tps)), [blk] * 3 + [pblk] * 2,
               pl.BlockSpec((None, N_CLS, BLK * tps // N_CLS, ST_W), lambda b, r, n: (b, 0, n, 0)),
               jax.ShapeDtypeStruct((B, N_CLS, m16, ST_W), F32),
               [pltpu.VMEM((tps, ST_W // 128, BLK, 128), F32)],
               "att_d1")(q, kb, vb, kb, vb)

    tps = ATT_TPS[1]
    g = N_CLS // 4
    per = BLK // g
    v5 = lambda a: a.reshape(B, g, 4, m16, a.shape[-1])
    blk = lambda w: pl.BlockSpec((None, g, None, per * tps, w), lambda b, r, n: (b, 0, r, n, 0))
    pblk = pl.BlockSpec((None, g, None, per, HW), lambda b, r, n: (b, 0, r, prev(n), 0))
    st4 = call(kern(has_prev=True, is_last=False, g=g, scatter_out=False),
               (B, 4, m16 // (per * tps)), [blk(HW)] * 3 + [pblk] * 2 + [blk(ST_W)], blk(ST_W),
               jax.ShapeDtypeStruct((B, g, 4, m16, ST_W), F32), [],
               "att_d4")(v5(qc), v5(kc), v5(vc), v5(kc), v5(vc), v5(st1))
    st4 = st4.reshape(B, N_CLS, m16, ST_W)

    tps = ATT_TPS[2]
    blk = lambda w: pl.BlockSpec((None, None, BLK * tps, w), lambda b, r, n: (b, r, n, 0))
    pblk = pl.BlockSpec((None, None, BLK, HW), lambda b, r, n: (b, r, prev(n), 0))
    return call(kern(has_prev=True, is_last=True, g=1, scatter_out=False),
                (B, N_CLS, m16 // (BLK * tps)),
                [blk(HW)] * 3 + [pblk] * 2 + [blk(ST_W), pl.BlockSpec((1, HW), lambda b, r, n: (0, 0))],
                blk(HW), jax.ShapeDtypeStruct((B, N_CLS, m16, HW), BF16), [],
                "att_d16")(qc, kc, vc, kc, vc, st4, gain)


def _att_dec_kernel(q_ref, kn_ref, vn_ref, k_ref, v_ref, gain_ref, out_ref):
    L = k_ref.shape[-1]
    dist = L - lax.broadcasted_iota(jnp.int32, (1, L), 1)
    mult = jnp.zeros((1, L), F32)
    for d in DILATIONS:
        mult = mult + jnp.where((dist & (d - 1)) == 0, jnp.where(dist <= d * N_BACK, 1.0, 0.0), 0.0)
    active = mult > 0.0
    q = q_ref[...]
    kn = kn_ref[...]
    vn = vn_ref[...]
    lane = lax.broadcasted_iota(jnp.int32, (HEAD_DIM, N_HEADS), 1)
    o = jnp.zeros((HEAD_DIM, N_HEADS), F32)
    for h in range(N_HEADS):
        qc = q[:, h:h + 1]
        s = jnp.where(active, jnp.sum(k_ref[h] * qc, axis=0, keepdims=True), -jnp.inf)
        s_new = jnp.sum(qc * kn[:, h:h + 1], axis=0, keepdims=True)
        m = jnp.maximum(jnp.max(s, axis=-1, keepdims=True), s_new)
        e = mult * jnp.exp(s - m)
        e_new = jnp.exp(s_new - m) * float(len(DILATIONS))
        l = jnp.sum(e, axis=-1, keepdims=True) + e_new
        acc = jnp.sum(v_ref[h] * e, axis=-1, keepdims=True) + e_new * vn[:, h:h + 1]
        o = jnp.where(lane == h, acc / l, o)
    ms = jnp.sum(jnp.sum(o * o, axis=-1, keepdims=True), axis=0, keepdims=True) * (1.0 / HW)
    out_ref[...] = o * lax.rsqrt(ms + RMS_EPS) * gain_ref[...]


def _att_decode(q, kn, vn, cache_kt, cache_vt, layer, gain):
    bd, L = cache_kt.shape[1], cache_kt.shape[-1]
    row = pl.BlockSpec((None, HEAD_DIM, N_HEADS), lambda b: (b, 0, 0))
    cblk = pl.BlockSpec((None, None, N_HEADS, HEAD_DIM, L), lambda b: (layer, b, 0, 0, 0))
    return pl.pallas_call(
        _att_dec_kernel,
        grid=(bd,),
        in_specs=[row, row, row, cblk, cblk, pl.BlockSpec((HEAD_DIM, N_HEADS), lambda b: (0, 0))],
        out_specs=row,
        out_shape=jax.ShapeDtypeStruct((bd, HEAD_DIM, N_HEADS), F32),
        compiler_params=_cparams("parallel"),
        name="att_dec",
    )(q, kn, vn, cache_kt, cache_vt, gain)


def _softplus(x):
    return jnp.maximum(x, 0.0) + jnp.log(1.0 + jnp.exp(-jnp.abs(x)))


def _rwkv_pre(z, vfirst, prm, has_vres):
    (dbase, up1, ibase, gup, vbase, vup, ksk, ksa, bd) = prm
    r = z[:, 0:HW]
    k = z[:, HW:2 * HW]
    v = z[:, 2 * HW:3 * HW]
    lo1 = z[:, 3 * HW:3 * HW + 128]
    lane = lax.broadcasted_iota(jnp.int32, lo1.shape, 1)
    lo1 = jnp.where(lane < LORA_DECAY, jnp.tanh(lo1), lo1)
    up = _mm(lo1, up1)
    w = -_softplus(-(dbase + up[:, 0:HW])) - 0.5
    logw = -jnp.exp(w)
    iclr = jax.nn.sigmoid(ibase + up[:, HW:2 * HW])
    g = _mm(jax.nn.sigmoid(z[:, 3 * HW + 128:3 * HW + 256]), gup)
    if has_vres:
        mix = jax.nn.sigmoid(vbase + _mm(z[:, 3 * HW + 256:3 * HW + 384], vup))
        v = v + (vfirst - v) * mix
    kk = k * ksk
    nrm = jnp.sqrt(_sum_heads(kk * kk, bd))
    kk = kk / jnp.maximum(nrm, 1e-12)
    k = k * (1.0 + (iclr - 1.0) * ksa)
    return r, k, v, logw, -kk, kk * iclr, g


def _rwkv_post(y, r, k, v, g, brk, gng, gnb, bd):
    mu = _sum_heads(y, bd) * (1.0 / HEAD_DIM)
    yc = y - mu
    var = _sum_heads(yc * yc, bd) * (1.0 / HEAD_DIM)
    yn = yc * lax.rsqrt(var + GN_EPS) * gng + gnb
    bonus = _sum_heads(r * k * brk, bd) * v
    return (yn + bonus) * g


def _bmm(a, b):
    return jnp.einsum('bij,bjk->bik', a.astype(BF16), b.astype(BF16), preferred_element_type=F32)


def _bmm_hi(a, b):
    ah, al = _split2(a)
    bh, bl = _split2(b)
    d = functools.partial(jnp.einsum, 'bij,bjk->bik', preferred_element_type=F32)
    return d(ah, bh) + d(ah, bl) + d(al, bh)


def _rwkv_chunk(at, rt, btT, ktT, vv, pcb, masks):
    strict, incl, eye, levels = masks
    c = CHUNK
    qr = jnp.concatenate([at, rt], axis=1)
    gb = _bmm(qr, btT)
    gk = _bmm(qr, ktT)
    a_ab = jnp.where(strict, gb[:, 0:c], 0.0)
    a_rb = jnp.where(incl, gb[:, c:2 * c], 0.0)
    a_ak = jnp.where(strict, gk[:, 0:c], 0.0)
    a_rk = jnp.where(incl, gk[:, c:2 * c], 0.0)
    t = eye + jnp.where(levels[0], a_ab, 0.0)
    for lv in levels[1:]:
        t = t + _bmm(_bmm(t, jnp.where(lv, a_ab, 0.0)), t)
    xv = _bmm(jnp.concatenate([a_ak, a_rk, ktT], axis=1), vv)
    w = _bmm(t, at)
    z = _bmm(t, xv[:, 0:c])
    lhs = jnp.concatenate([a_rb, btT], axis=1)
    lw = _bmm(lhs, w)
    lz = _bmm(lhs, z)
    rp = rt + lw[:, 0:c]
    y0 = lz[:, 0:c] + xv[:, c:2 * c]
    m = pcb * (eye + lw[:, c:2 * c])
    n = pcb * (lz[:, c:2 * c] + xv[:, 2 * c:3 * c])
    return rp, y0, m, n


def _chunk_masks():
    ri = lax.broadcasted_iota(jnp.int32, (CHUNK, CHUNK), 0)
    ci = lax.broadcasted_iota(jnp.int32, (CHUNK, CHUNK), 1)
    strict = ri > ci
    incl = ri >= ci
    eye = jnp.where(ri == ci, 1.0, 0.0).astype(F32)
    levels = []
    size = 2
    while size <= CHUNK:
        same = (ri // size) == (ci // size)
        if size == 2:
            levels.append(same)
        else:
            inner = (ri // (size // 2)) == (ci // (size // 2))
            levels.append(jnp.where(same, jnp.where(inner, 0, 1), 0) > 0)
        size *= 2
    return strict, incl, eye, levels


def _rwkv_kernel(*refs, has_vres, tb):
    pr_ref = refs[0]
    pos = 1
    vf_ref = None
    if has_vres:
        vf_ref = refs[pos]
        pos += 1
    (mu_ref, dbase, up1, ibase, gup, vbase, vup, ksk, ksa, brk, gng, gnb, bd_ref, tri_ref) = refs[pos:pos + 14]
    pos += 14
    rw_ref = refs[pos]
    pos += 1
    vfo_ref = None
    if not has_vres:
        vfo_ref = refs[pos]
        pos += 1
    so_ref = refs[pos]
    pos += 1
    h_scr, carry, at4, rt4, bt4, kt4, v4, pc4, y4, y_s = refs[pos:pos + 10]
    nc = tb // CHUNK

    i = pl.program_id(1)

    @pl.when(i == 0)
    def _():
        h_scr[...] = jnp.zeros_like(h_scr)
        carry[...] = jnp.zeros_like(carry)

    p = pr_ref[...]
    row = lax.broadcasted_iota(jnp.int32, p.shape, 0)
    prev = jnp.where(row == 0, carry[0:1, :], pltpu.roll(p, 1, 0))
    carry[0:1, :] = p[tb - 1:tb, :]
    z = p + (prev - p) * mu_ref[...]
    bd = bd_ref[...]
    prm = (dbase[...], up1[...], ibase[...], gup[...], vbase[...], vup[...], ksk[...], ksa[...], bd)
    r, k, v, logw, a, b, g = _rwkv_pre(z, vf_ref[...] if has_vres else None, prm, has_vres)
    if not has_vres:
        vfo_ref[...] = v
    hi = logw.astype(BF16)
    r1 = logw - hi.astype(F32)
    mid = r1.astype(BF16)
    low = (r1 - mid.astype(F32)).astype(BF16)
    tri = tri_ref[...]
    d = functools.partial(jnp.dot, preferred_element_type=F32)
    cum = d(tri, hi) + d(tri, mid) + d(tri, low)
    ecum = jnp.exp(cum)
    einv = jnp.exp(-cum)
    a_t = a * jnp.exp(cum - logw)
    r_t = r * ecum
    btT = (b * einv).T
    ktT = (k * einv).T
    pcT = ecum.T
    for c in range(nc):
        rs = slice(c * CHUNK, (c + 1) * CHUNK)
        for h in range(N_HEADS):
            ls = slice(h * HEAD_DIM, (h + 1) * HEAD_DIM)
            at4[c, h] = a_t[rs, ls]
            rt4[c, h] = r_t[rs, ls]
            v4[c, h] = v[rs, ls]
            bt4[c, h] = btT[ls, rs]
            kt4[c, h] = ktT[ls, rs]
            pc4[c, h] = jnp.broadcast_to(pcT[ls, (c + 1) * CHUNK - 1:(c + 1) * CHUNK], (HEAD_DIM, CHUNK))
    masks = _chunk_masks()

    def chunks(i, _):
        c0 = i * RWKV_GROUP
        grp = lambda ref: ref[pl.ds(c0, RWKV_GROUP)].reshape(RWKV_GROUP * N_HEADS, HEAD_DIM, HEAD_DIM)
        rp, y0, m, n = _rwkv_chunk(grp(at4), grp(rt4), grp(bt4), grp(kt4), grp(v4), grp(pc4), masks)
        h = h_scr[...]
        for j in range(RWKV_GROUP):
            hs = slice(j * N_HEADS, (j + 1) * N_HEADS)
            y4[c0 + j] = _bmm(rp[hs], h) + y0[hs]
            h = _bmm_hi(m[hs], h) + n[hs]
        h_scr[...] = h
        return 0

    lax.fori_loop(0, nc // RWKV_GROUP, chunks, 0)
    for c in range(nc):
        for h in range(N_HEADS):
            y_s[c * CHUNK:(c + 1) * CHUNK, h * HEAD_DIM:(h + 1) * HEAD_DIM] = y4[c, h]
    rw = _rwkv_post(y_s[...], r, k, v, g, brk[...], gng[...], gnb[...], bd)
    rw_ref[...] = rw.astype(BF16)

    @pl.when(i == pl.num_programs(1) - 1)
    def _():
        hst = h_scr[...]
        hi = hst.astype(BF16)
        r1 = hst - hi.astype(F32)
        mid = r1.astype(BF16)
        low = (r1 - mid.astype(F32)).astype(BF16)
        eye = jnp.broadcast_to(masks[2].astype(BF16)[None], (N_HEADS, HEAD_DIM, HEAD_DIM))
        d = functools.partial(jnp.einsum, 'bik,bjk->bij', preferred_element_type=F32)
        so_ref[...] = d(eye, hi) + d(eye, mid) + d(eye, low)


def _rwkv_prompt(pr, vfirst, wts, tb):
    B, T, _ = pr.shape
    has_vres = vfirst is not None
    blk = lambda w: pl.BlockSpec((None, tb, w), lambda b, i: (b, i, 0))
    const = lambda a: pl.BlockSpec(a.shape, lambda b, i: (0,) * a.ndim)
    args, specs = [pr], [blk(PR_W)]
    if has_vres:
        args.append(vfirst)
        specs.append(blk(HW))
    idx = jnp.arange(tb)
    tri = ((idx[:, None] // CHUNK == idx[None, :] // CHUNK) & (idx[:, None] >= idx[None, :])).astype(BF16)
    for a in wts + (tri,):
        args.append(a)
        specs.append(const(a))
    out_shape = [jax.ShapeDtypeStruct((B, T, HW), BF16)]
    out_specs = [blk(HW)]
    if not has_vres:
        out_shape.append(jax.ShapeDtypeStruct((B, T, HW), F32))
        out_specs.append(blk(HW))
    out_shape.append(jax.ShapeDtypeStruct((B, N_HEADS, HEAD_DIM, HEAD_DIM), F32))
    out_specs.append(pl.BlockSpec((None, N_HEADS, HEAD_DIM, HEAD_DIM), lambda b, i: (b, 0, 0, 0)))
    per_head = lambda: pltpu.VMEM((tb // CHUNK, N_HEADS, HEAD_DIM, HEAD_DIM), F32)
    scratch = [pltpu.VMEM((N_HEADS, HEAD_DIM, HEAD_DIM), F32), pltpu.VMEM((8, PR_W), F32)]
    scratch += [per_head() for _ in range(7)] + [pltpu.VMEM((tb, HW), F32)]
    outs = pl.pallas_call(
        functools.partial(_rwkv_kernel, has_vres=has_vres, tb=tb),
        grid=(B, T // tb),
        in_specs=specs,
        out_specs=out_specs,
        out_shape=out_shape,
        scratch_shapes=scratch,
        compiler_params=_cparams("parallel", "arbitrary"),
        name="rwkv",
    )(*args)
    if has_vres:
        return outs[0], vfirst, outs[1]
    return outs[0], outs[1], outs[2]


def _rwkv_dec_kernel(*refs, has_vres):
    pr_ref, pv_ref = refs[0:2]
    pos = 2
    vf_ref = None
    if has_vres:
        vf_ref = refs[pos]
        pos += 1
    s_ref = refs[pos]
    pos += 1
    (mu_ref, dbase, up1, ibase, gup, vbase, vup, ksk, ksa, brk, gng, gnb, bd_ref) = refs[pos:pos + 13]
    pos += 13
    rw_ref = refs[pos]
    pos += 1
    vfo_ref = None
    if not has_vres:
        vfo_ref = refs[pos]
        pos += 1
    so_ref, y_s = refs[pos:pos + 2]

    p = pr_ref[...]
    z = p + (pv_ref[...] - p) * mu_ref[...]
    bd = bd_ref[...]
    prm = (dbase[...], up1[...], ibase[...], gup[...], vbase[...], vup[...], ksk[...], ksa[...], bd)
    r, k, v, logw, a, b, g = _rwkv_pre(z, vf_ref[...] if has_vres else None, prm, has_vres)
    if not has_vres:
        vfo_ref[...] = v
    w = jnp.exp(logw)
    ri = lax.broadcasted_iota(jnp.int32, (HEAD_DIM, HEAD_DIM), 0)
    ci = lax.broadcasted_iota(jnp.int32, (HEAD_DIM, HEAD_DIM), 1)
    eye = jnp.where(ri == ci, 1.0, 0.0).astype(F32)
    for i0 in range(0, p.shape[0], 2):
        idx = [(i, h, slice(i, i + 1), slice(h * HEAD_DIM, (h + 1) * HEAD_DIM))
               for i in range(i0, i0 + 2) for h in range(N_HEADS)]
        s0 = [s_ref[i, h] for i, h, _, _ in idx]
        sa = [jnp.sum(s * a[row, sl], axis=-1, keepdims=True) for s, (_, _, row, sl) in zip(s0, idx)]
        vcol = [jnp.sum(eye * v[row, sl], axis=-1, keepdims=True) for _, _, row, sl in idx]
        s1 = [s * w[row, sl] + x * b[row, sl] + vc * k[row, sl]
              for s, x, vc, (_, _, row, sl) in zip(s0, sa, vcol, idx)]
        for s, (i, h, _, _) in zip(s1, idx):
            so_ref[i, h] = s
        ycol = [jnp.sum(s * r[row, sl], axis=-1, keepdims=True) for s, (_, _, row, sl) in zip(s1, idx)]
        for yc, (_, _, row, sl) in zip(ycol, idx):
            y_s[row, sl] = jnp.sum(eye * yc, axis=0, keepdims=True)
    rw_ref[...] = _rwkv_post(y_s[...], r, k, v, g, brk[...], gng[...], gnb[...], bd)


DEC_ROWS = 8


def _rwkv_decode(pr, prev, vfirst, state, layer, wts):
    bd_ = pr.shape[0]
    rows = DEC_ROWS
    has_vres = vfirst is not None
    row = lambda w: pl.BlockSpec((rows, w), lambda b: (b, 0))
    const = lambda a: pl.BlockSpec(a.shape, lambda b: (0,) * a.ndim)
    sblk = (rows, N_HEADS, HEAD_DIM, HEAD_DIM)
    args, specs = [pr, prev], [row(PR_W), row(PR_W)]
    if has_vres:
        args.append(vfirst)
        specs.append(row(HW))
    args.append(state)
    specs.append(pl.BlockSpec((None,) + sblk, lambda b: (layer, b, 0, 0, 0)))
    for a in wts:
        args.append(a)
        specs.append(const(a))
    out_shape = [jax.ShapeDtypeStruct((bd_, HW), F32)]
    out_specs = [row(HW)]
    if not has_vres:
        out_shape.append(jax.ShapeDtypeStruct((bd_, HW), F32))
        out_specs.append(row(HW))
    out_shape.append(jax.ShapeDtypeStruct((bd_, N_HEADS, HEAD_DIM, HEAD_DIM), F32))
    out_specs.append(pl.BlockSpec(sblk, lambda b: (b, 0, 0, 0)))
    outs = pl.pallas_call(
        functools.partial(_rwkv_dec_kernel, has_vres=has_vres),
        grid=(bd_ // rows,),
        in_specs=specs,
        out_specs=out_specs,
        out_shape=out_shape,
        scratch_shapes=[pltpu.VMEM((rows, HW), F32)],
        compiler_params=_cparams("parallel"),
        name="rwkv_dec",
    )(*args)
    if has_vres:
        return outs[0], vfirst, outs[1]
    return outs[0], outs[1], outs[2]


def _mix_ffn_kernel(att_ref, rw_ref, x_ref, wo_ref, g1_ref, b1_ref, wu_ref, wd_ref, g2_ref, b2_ref,
                    out_ref, *scr):
    d = functools.partial(jnp.dot, preferred_element_type=F32)
    if scr:
        nat, = scr
        rows = att_ref.shape[1]
        for r in range(N_CLS):
            blk = att_ref[r].astype(F32)
            for j in range(nat.shape[0]):
                nat[j, pl.ds(r, rows, stride=N_CLS), :] = blk[:, j * 128:(j + 1) * 128]
        att = jnp.concatenate([nat[j] for j in range(nat.shape[0])], axis=1).astype(BF16)
    else:
        att = att_ref[...]
    mix = d(att, wo_ref[0:HW, :]) + d(rw_ref[...], wo_ref[HW:2 * HW, :])
    x1 = _layer_norm(ALPHA * x_ref[...] + mix, g1_ref[...], b1_ref[...])
    xb = x1.astype(BF16)
    acc = ALPHA * x1
    for j in range(D_FF // FF_CHUNK):
        cs = slice(j * FF_CHUNK, (j + 1) * FF_CHUNK)
        h = jnp.maximum(d(xb, wu_ref[:, cs]), 0.0)
        acc = acc + d((h * h).astype(BF16), wd_ref[cs, :])
    out_ref[...] = _layer_norm(acc, g2_ref[...], b2_ref[...])


FF_CHUNK = 1024


def _mix_ffn(att, rw, x, w_out, w_up, w_down, layer, ln, tm):
    n = x.shape[0]
    row = lambda w: pl.BlockSpec((tm, w), lambda i: (i, 0))
    vec = pl.BlockSpec((None, 1, D_MODEL), lambda i: (layer, 0, 0))
    wgt = lambda r, c: pl.BlockSpec((None, r, c), lambda i: (layer, 0, 0), pipeline_mode=pl.Buffered(1))
    scratch = []
    att_spec = row(HW)
    if att.ndim == 4:
        tpb = att.shape[2] * N_CLS // tm
        att_spec = pl.BlockSpec((None, N_CLS, tm // N_CLS, HW), lambda i: (i // tpb, 0, i % tpb, 0))
        scratch = [pltpu.VMEM((HW // 128, tm, 128), F32)]
    return pl.pallas_call(
        _mix_ffn_kernel,
        grid=(n // tm,),
        in_specs=[att_spec, row(HW), row(D_MODEL), wgt(D_MODEL, D_MODEL), vec, vec,
                  wgt(D_MODEL, D_FF), wgt(D_FF, D_MODEL), vec, vec],
        out_specs=row(D_MODEL),
        out_shape=jax.ShapeDtypeStruct((n, D_MODEL), F32),
        scratch_shapes=scratch,
        compiler_params=_cparams("parallel"),
        name="mix_ffn",
    )(att, rw, x, w_out, ln[0], ln[1], w_up, w_down, ln[2], ln[3])


def _rope_tables(pos):
    half = HEAD_DIM // 2
    inv = ROPE_THETA ** (-jnp.arange(half, dtype=F32) * (2.0 / HEAD_DIM))
    ang = pos.astype(F32)[:, None] * inv[None, :]
    cos, sin = jnp.cos(ang), jnp.sin(ang)
    cos2 = jnp.concatenate([cos, cos, cos, cos], axis=1)
    sin2 = jnp.concatenate([-sin, sin, -sin, sin], axis=1)
    return cos2, sin2


def kernel(x_prompt, x_sample, state_shift, state_wkv, cache_k, cache_v, w_in, w_vres_in, shift_mu, vres_mu, decay_base, decay_up, iclr_base, iclr_up, gate_up, vres_base, vres_up, key_scale_k, key_scale_a, bonus_rk, gn_g, gn_b, att_gain, w_out, ln1_g, ln1_b, w_ff_up, w_ff_down, ln2_g, ln2_b):
    B, T, _ = x_prompt.shape
    bd_, ds_, _ = x_sample.shape
    L = cache_k.shape[2]
    past_len = 8192
    assert ds_ == 1 and T % (N_CLS * BLK * ATT_TPS[2]) == 0 and T % (BLK * max(ATT_TPS)) == 0
    keep = min(DILATIONS[-1] * N_BACK, T)
    tm_p, tb = 512, 256

    z32 = jnp.zeros((1, D_MODEL, LORA_VRES), F32)
    w_ext = jnp.concatenate([w_in, jnp.concatenate([z32, w_vres_in], 0),
                             jnp.zeros((DEPTH, D_MODEL, LO_W - 288), F32)], axis=-1).astype(BF16)
    mu_ext = jnp.concatenate([shift_mu, jnp.concatenate([jnp.zeros((1, LORA_VRES), F32), vres_mu], 0),
                              jnp.zeros((DEPTH, LO_W - 288), F32)], axis=-1)
    zdu = jnp.zeros((DEPTH, LORA_DECAY, HW), F32)
    up1 = jnp.concatenate([jnp.concatenate([decay_up, zdu], -1), jnp.concatenate([zdu, iclr_up], -1)],
                          axis=1).astype(BF16)
    gup = gate_up.astype(BF16)
    vup = jnp.concatenate([jnp.concatenate([jnp.zeros((1, LORA_VRES, HW), F32), vres_up], 0),
                           jnp.zeros((DEPTH, 128 - LORA_VRES, HW), F32)], axis=1).astype(BF16)
    vbase = jnp.concatenate([jnp.zeros((1, HW), F32), vres_base], 0)
    hid = jnp.arange(HW) // HEAD_DIM
    bd = (hid[:, None] == hid[None, :]).astype(BF16)
    w_out_b = w_out.astype(BF16)
    w_up_b = w_ff_up.astype(BF16)
    w_down_b = w_ff_down.astype(BF16)
    r3 = lambda a: a.reshape(DEPTH, 1, a.shape[-1])
    ln = (r3(ln1_g), r3(ln1_b), r3(ln2_g), r3(ln2_b))

    cos_p, sin_p = _rope_tables(jnp.arange(T))
    cos_s, sin_s = _rope_tables(jnp.full((bd_,), past_len))
    cache_kt = jnp.transpose(cache_k, (0, 1, 3, 4, 2))
    cache_vt = jnp.transpose(cache_v, (0, 1, 3, 4, 2))

    hp = x_prompt.reshape(B * T, D_MODEL)
    hs = x_sample.reshape(bd_, D_MODEL)
    vf_p = vf_s = None
    shift_p, shift_s, wkv_p, wkv_s, kp_l, vp_l, ks_l, vs_l = [], [], [], [], [], [], [], []
    for l in range(DEPTH):
        row1 = lambda a: a[l].reshape(1, -1)
        wts = (row1(mu_ext), row1(decay_base), up1[l], row1(iclr_base), gup[l], row1(vbase), vup[l],
               row1(key_scale_k), row1(key_scale_a), row1(bonus_rk), row1(gn_g), row1(gn_b), bd)
        gain = row1(att_gain)
        shift_p.append(hp.reshape(B, T, D_MODEL)[:, -1])
        shift_s.append(hs)

        q, kb, vb, kf, vf, pr, qc, kc, vc = _proj(hp, w_ext, l, cos_p, sin_p, tm_p, seq_len=T, keep=keep)
        r3d = lambda a: a.reshape(B, T, a.shape[-1])
        att = _att_prompt(r3d(q), r3d(kb), r3d(vb), qc, kc, vc, gain)
        rw, vf_p, s_p = _rwkv_prompt(r3d(pr), vf_p, wts, tb)
        hp = _mix_ffn(att, rw.reshape(B * T, HW), hp, w_out_b, w_up_b, w_down_b, l, ln, 512)
        wkv_p.append(s_p)
        kp_l.append(kf.reshape(B, keep, N_HEADS, HEAD_DIM))
        vp_l.append(vf.reshape(B, keep, N_HEADS, HEAD_DIM))

        qs, kbs, vbs, kfs, vfs, prs = _proj(hs, w_ext, l, cos_s, sin_s, bd_)
        prev = _proj_rwkv(state_shift[l], w_ext, l)
        e1 = lambda a: a.reshape(bd_, 1, a.shape[-1])
        hd = lambda a: a.astype(F32).reshape(-1, N_HEADS, HEAD_DIM).swapaxes(-1, -2)
        att_s = _att_decode(hd(qs), hd(kfs), hd(vfs), cache_kt, cache_vt, l, hd(gain)[0])
        att_s = att_s.swapaxes(-1, -2)
        rw_s, vf_s, s_s = _rwkv_decode(prs, prev, vf_s, state_wkv, l, wts)
        hs = _mix_ffn(att_s.reshape(bd_, HW).astype(BF16), rw_s.astype(BF16), hs, w_out_b, w_up_b, w_down_b,
                      l, ln, bd_)
        wkv_s.append(s_s)
        ks_l.append(kfs.reshape(bd_, 1, N_HEADS, HEAD_DIM))
        vs_l.append(vfs.reshape(bd_, 1, N_HEADS, HEAD_DIM))

    return (hp.reshape(B, T, D_MODEL), hs.reshape(bd_, 1, D_MODEL), jnp.stack(shift_p), jnp.stack(shift_s),
            jnp.stack(wkv_p), jnp.stack(wkv_s), jnp.stack(kp_l), jnp.stack(vp_l), jnp.stack(ks_l),
            jnp.stack(vs_l))
```
